```python
import math
import jax, jax.numpy as jnp
from jax import lax
import numpy as np

D_MODEL = 1024
BATCH = 2
SEQ = 16384
DEPTH = 2

N_MIXERS = 2
N_HYENA_LAYERS = (DEPTH + N_MIXERS - 1) // N_MIXERS
N_NA_LAYERS = DEPTH // N_MIXERS
GRID_W = 64
RMS_EPS = 1e-6
HY_SHORT_CONV = 3
HY_EMB_DIM = 33
HY_N_BANDS = (HY_EMB_DIM - 1) // 2
HY_FILTER_WIDTH = 64
HY_FAST_DECAY_PCT = 0.3
HY_SLOW_DECAY_PCT = 1.5
HY_DECAY_TARGET = 1e-2
NA_HEADS = 16
NA_HEAD_DIM = D_MODEL // NA_HEADS
NA_KH = 8
NA_KW = 16
FFN_HIDDEN = -(-8 * D_MODEL // (3 * 256)) * 256

kernel_name = "hybrid_hyena_natten_encoder"


def rmsnorm(x, g):
    x32 = x.astype(jnp.float32)
    y = x32 * lax.rsqrt(jnp.mean(x32 * x32, axis=-1, keepdims=True) + RMS_EPS)
    return (y * g.astype(jnp.float32)).astype(x.dtype)


def short_conv(z, w, b):
    c = z.shape[-1]
    pad = HY_SHORT_CONV // 2
    y = lax.conv_general_dilated(
        z, w[:, None, :].astype(z.dtype), window_strides=(1,), padding=((pad, pad),),
        dimension_numbers=("NWC", "WIO", "NWC"), feature_group_count=c)
    return y + b


def hyena_filter(L, w1, b1, w2, b2, w3, b3, freq, w_out, decay):
    f32 = jnp.float32
    t = jnp.linspace(0.0, 1.0, L, dtype=f32)[:, None]
    w = 2.0 * math.pi * jnp.arange(L, dtype=f32)[:, None] / L
    bands = jnp.linspace(1e-4, HY_N_BANDS - 1, HY_N_BANDS, dtype=f32)[None, :]
    feat = jnp.concatenate([t, jnp.cos(bands * w), -jnp.sin(bands * w)], axis=-1)
    fr = freq.astype(f32)
    act = lambda a: jnp.sin(fr * a)
    h = act(feat @ w1.astype(f32) + b1.astype(f32))
    h = act(h @ w2.astype(f32) + b2.astype(f32))
    h = act(h @ w3.astype(f32) + b3.astype(f32))
    h = (h @ w_out.astype(f32)).reshape(L, 2, D_MODEL)
    h = h * jnp.exp(-t[:, :, None] * jnp.abs(decay.astype(f32)))
    g = jnp.concatenate([h[:, 0], jnp.zeros((1, D_MODEL), f32), jnp.flip(h[1:, 1], axis=0)], axis=0)
    return g / jnp.sum(jnp.abs(g), axis=0, keepdims=True)


def hyena_mixer(u, w_in, b_in, conv_w, conv_b, f_w1, f_b1, f_w2, f_b2, f_w3, f_b3,
                f_freq, f_wout, decay, skip, w_out, b_out):
    L = u.shape[1]
    z = short_conv(u @ w_in + b_in, conv_w, conv_b)
    x0, x1, v = jnp.split(z, 3, axis=-1)
    s = (v * x1).astype(jnp.float32)
    g = hyena_filter(L, f_w1, f_b1, f_w2, f_b2, f_w3, f_b3, f_freq, f_wout, decay)
    n = 2 * L
    y = jnp.fft.irfft(jnp.fft.rfft(s, n=n, axis=1) * jnp.fft.rfft(g, n=n, axis=0)[None],
                      n=n, axis=1)[:, :L]
    y = (y + s * skip.astype(jnp.float32)).astype(u.dtype) * x0
    return y @ w_out + b_out


def na_mixer(u, w_qkv, b_qkv, rpb, w_o, b_o):
    B_, L, D = u.shape
    rows = L // GRID_W
    kh = min(NA_KH, rows)
    kw = NA_KW
    qkv = u @ w_qkv + b_qkv
    q, k, v = [a.reshape(B_, rows, GRID_W, NA_HEADS, NA_HEAD_DIM) for a in jnp.split(qkv, 3, axis=-1)]
    q = q * (NA_HEAD_DIM ** -0.5)
    col = jnp.arange(GRID_W)
    col_start = jnp.clip(col - kw // 2, 0, GRID_W - kw)
    col_idx = col_start[:, None] + jnp.arange(kw)[None, :]
    col_off = col_idx - col[:, None] + (NA_KW - 1)

    def row_block(r):
        r_start = jnp.clip(r - kh // 2, 0, rows - kh)
        k_rows = lax.dynamic_slice_in_dim(k, r_start, kh, axis=1)
        v_rows = lax.dynamic_slice_in_dim(v, r_start, kh, axis=1)
        k_win = k_rows[:, :, col_idx]
        v_win = v_rows[:, :, col_idx]
        q_r = lax.dynamic_index_in_dim(q, r, axis=1, keepdims=False)
        s = jnp.einsum("bwhd,biwjhd->bhwij", q_r, k_win)
        row_off = r_start + jnp.arange(kh) - r + (NA_KH - 1)
        bias = rpb[:, row_off[None, :, None], col_off[:, None, :]]
        s = (s + bias[None]).astype(jnp.float32).reshape(B_, NA_HEADS, GRID_W, kh * kw)
        p = jax.nn.softmax(s, axis=-1).reshape(B_, NA_HEADS, GRID_W, kh, kw).astype(v.dtype)
        return jnp.einsum("bhwij,biwjhd->bwhd", p, v_win)

    out = lax.map(row_block, jnp.arange(rows))
    out = jnp.transpose(out, (1, 0, 2, 3, 4)).reshape(B_, L, D)
    return out @ w_o + b_o


def swiglu(x, w_gate, w_up, w_down):
    return (jax.nn.silu(x @ w_gate) * (x @ w_up)) @ w_down


def setup_inputs(seed: int = 0) -> dict:
    key = jax.random.key(seed)
    ks = iter(jax.random.split(key, 40))
    nrm = lambda shape, scale: scale * jax.random.normal(next(ks), shape, jnp.float32)
    D, F, NH, NN = D_MODEL, FFN_HIDDEN, N_HYENA_LAYERS, N_NA_LAYERS
    base_decay = jnp.abs(jnp.linspace(math.log(HY_DECAY_TARGET) / HY_SLOW_DECAY_PCT,
                                      math.log(HY_DECAY_TARGET) / HY_FAST_DECAY_PCT, D, dtype=jnp.float32))
    return {
        "x": nrm((BATCH, SEQ, D), 1.0),
        "norm_mix": 1.0 + nrm((DEPTH, D), 0.02),
        "norm_ffn": 1.0 + nrm((DEPTH, D), 0.02),
        "norm_final": 1.0 + nrm((D,), 0.02),
        "hy_w_in": nrm((NH, D, 3 * D), D ** -0.5),
        "hy_b_in": nrm((NH, 3 * D), 0.02),
        "hy_conv_w": nrm((NH, HY_SHORT_CONV, 3 * D), HY_SHORT_CONV ** -0.5),
        "hy_conv_b": nrm((NH, 3 * D), 0.02),
        "hy_f_w1": nrm((NH, HY_EMB_DIM, HY_FILTER_WIDTH), HY_EMB_DIM ** -0.5),
        "hy_f_b1": nrm((NH, HY_FILTER_WIDTH), 0.02),
        "hy_f_w2": nrm((NH, HY_FILTER_WIDTH, HY_FILTER_WIDTH), HY_FILTER_WIDTH ** -0.5),
        "hy_f_b2": nrm((NH, HY_FILTER_WIDTH), 0.02),
        "hy_f_w3": nrm((NH, HY_FILTER_WIDTH, HY_FILTER_WIDTH), HY_FILTER_WIDTH ** -0.5),
        "hy_f_b3": nrm((NH, HY_FILTER_WIDTH), 0.02),
        "hy_f_freq": 1.0 + nrm((NH, HY_FILTER_WIDTH), 0.02),
        "hy_f_wout": nrm((NH, HY_FILTER_WIDTH, 2 * D), HY_FILTER_WIDTH ** -0.5),
        "hy_decay": base_decay * (1.0 + nrm((NH, 2, D), 0.05)),
        "hy_skip": nrm((NH, D), 0.5),
        "hy_w_out": nrm((NH, D, D), D ** -0.5),
        "hy_b_out": nrm((NH, D), 0.02),
        "na_w_qkv": nrm((NN, D, 3 * D), D ** -0.5),
        "na_b_qkv": nrm((NN, 3 * D), 0.02),
        "na_rpb": nrm((NN, NA_HEADS, 2 * NA_KH - 1, 2 * NA_KW - 1), 0.02),
        "na_w_o": nrm((NN, D, D), D ** -0.5),
        "na_b_o": nrm((NN, D), 0.02),
        "ffn_w_gate": nrm((DEPTH, D, F), D ** -0.5),
        "ffn_w_up": nrm((DEPTH, D, F), D ** -0.5),
        "ffn_w_down": nrm((DEPTH, F, D), F ** -0.5),
    }


def reference(x, norm_mix, norm_ffn, norm_final,
              hy_w_in, hy_b_in, hy_conv_w, hy_conv_b, hy_f_w1, hy_f_b1, hy_f_w2, hy_f_b2,
              hy_f_w3, hy_f_b3, hy_f_freq, hy_f_wout, hy_decay, hy_skip, hy_w_out, hy_b_out,
              na_w_qkv, na_b_qkv, na_rpb, na_w_o, na_b_o,
              ffn_w_gate, ffn_w_up, ffn_w_down):
    for i in range(DEPTH):
        h = rmsnorm(x, norm_mix[i])
        j = i // N_MIXERS
        if i % N_MIXERS == 0:
            mixed = hyena_mixer(h, hy_w_in[j], hy_b_in[j], hy_conv_w[j], hy_conv_b[j],
                                hy_f_w1[j], hy_f_b1[j], hy_f_w2[j], hy_f_b2[j], hy_f_w3[j], hy_f_b3[j],
                                hy_f_freq[j], hy_f_wout[j], hy_decay[j], hy_skip[j],
                                hy_w_out[j], hy_b_out[j])
        else:
            mixed = na_mixer(h, na_w_qkv[j], na_b_qkv[j], na_rpb[j], na_w_o[j], na_b_o[j])
        x = x + mixed
        x = x + swiglu(rmsnorm(x, norm_ffn[i]), ffn_w_gate[i], ffn_w_up[i], ffn_w_down[i])
    return rmsnorm(x, norm_final)
```

```python
import functools
import math

import numpy as np
import jax
import jax.numpy as jnp
from jax import lax
from jax.experimental import pallas as pl
from jax.experimental.pallas import tpu as pltpu

F32 = jnp.float32
BF16 = jnp.bfloat16

D_MODEL = 1024
SEQ = 16384
BATCH = 2
FFN_HIDDEN = 2816
GRID_W = 64
RMS_EPS = 1e-6
HY_EMB_DIM = 33
HY_EMB_PAD = 40
HY_N_BANDS = 16
HY_WIDTH = 64
NA_HEADS = 16
NA_HEAD_DIM = 64
NA_KH = 8
NA_KW = 16

FFT_N = 2 * SEQ
FFT_N1 = 256
FFT_N2 = 128
FFT_H1 = FFT_N1 // 2

NA_ROWS = SEQ // GRID_W
NA_QR = 4
NA_KR = 12
NA_NBLK = NA_ROWS // NA_QR
NEG_BIG = -1e30

VMEM_LIMIT = 56 * 1024 * 1024


def _cparams(sem):
    return pltpu.CompilerParams(dimension_semantics=sem, vmem_limit_bytes=VMEM_LIMIT)


def _const_spec(shape):
    nd = len(shape)
    return pl.BlockSpec(shape, lambda *_: (0,) * nd, pipeline_mode=pl.Buffered(1))


def _rms(x, g):
    ms = jnp.mean(x * x, axis=-1, keepdims=True)
    return x * lax.rsqrt(ms + RMS_EPS) * g


@functools.lru_cache(maxsize=None)
def _fft_tables():
    n1, n2, n = FFT_N1, FFT_N2, FFT_N
    k1 = np.arange(n1)[:, None]
    nh = np.arange(FFT_H1)[None, :]
    nf = np.arange(n1)[None, :]
    c = np.exp(-2j * np.pi * (nh * k1) / n1)
    m1 = np.block([[c.real, -c.imag], [c.imag, c.real]])
    cf = np.exp(-2j * np.pi * (nf * k1) / n1)
    m1g = np.concatenate([cf.real, cf.imag], 0)
    e = np.exp(2j * np.pi * (nh * k1) / n1).T
    m3 = np.block([[e.real, -e.imag], [e.imag, e.real]])
    a = np.arange(n2)
    f = np.exp(-2j * np.pi * np.outer(a, a) / n2)
    m2 = np.block([[f.real, -f.imag], [f.imag, f.real]])
    m2c = np.block([[f.real, f.imag], [-f.imag, f.real]])
    tw = np.exp(-2j * np.pi * np.outer(np.arange(n1), np.arange(n2)) / n)
    as_bf = lambda m: np.asarray(m, dtype=np.float32).astype(BF16)
    as_f = lambda m: np.asarray(m, dtype=np.float32)
    return dict(m1=as_bf(m1), m1g=as_bf(m1g), m3=as_bf(m3), m2=as_bf(m2), m2c=as_bf(m2c),
                twc=as_f(tw.real), tws=as_f(tw.imag),
                twc_t=as_f(tw.real.T), tws_t=as_f(tw.imag.T))


@functools.lru_cache(maxsize=None)
def _filter_features():
    ell = SEQ
    t = np.linspace(0.0, 1.0, ell, dtype=np.float32).astype(np.float64)
    w = (2.0 * math.pi * np.arange(ell, dtype=np.float32) / ell).astype(np.float32).astype(np.float64)
    bands = np.linspace(1e-4, HY_N_BANDS - 1, HY_N_BANDS, dtype=np.float32).astype(np.float64)
    ang = (bands[None, :] * w[:, None]).astype(np.float32).astype(np.float64)
    feat = np.concatenate([t[:, None], np.cos(ang), -np.sin(ang)], axis=-1)
    tab = np.zeros((2 * ell, HY_EMB_PAD), np.float64)
    tab[:ell, :HY_EMB_DIM] = feat
    tab[:ell, HY_EMB_DIM] = 1.0
    tab[ell + 1:, :HY_EMB_DIM] = feat[:0:-1]
    tab[ell + 1:, HY_EMB_DIM] = 1.0
    return np.asarray(tab, dtype=np.float32)


@functools.lru_cache(maxsize=None)
def _na_bias_index():
    types = (0, 1, NA_NBLK - 1)
    ro = np.zeros((3, NA_QR, GRID_W, NA_KR, GRID_W), np.int32)
    co = np.zeros_like(ro)
    ok = np.zeros(ro.shape, bool)
    c = np.arange(GRID_W)
    cs = np.clip(c - NA_KW // 2, 0, GRID_W - NA_KW)
    kc = np.arange(GRID_W)
    jj = kc[None, :] - cs[:, None]
    col_ok = (jj >= 0) & (jj < NA_KW)
    col_off = kc[None, :] - c[:, None] + (NA_KW - 1)
    for ti, j in enumerate(types):
        base = NA_QR * min(max(j - 1, 0), NA_NBLK - 3)
        for ri in range(NA_QR):
            r = NA_QR * j + ri
            rs = min(max(r - NA_KH // 2, 0), NA_ROWS - NA_KH)
            for kri in range(NA_KR):
                kr = base + kri
                i = kr - rs
                row_ok = 0 <= i < NA_KH
                ro[ti, ri, :, kri, :] = np.clip(kr - r + (NA_KH - 1), 0, 2 * NA_KH - 2)
                co[ti, ri, :, kri, :] = np.clip(col_off, 0, 2 * NA_KW - 2)
                ok[ti, ri, :, kri, :] = col_ok & row_ok
    return ro, co, ok


def _hy_in_kernel(xp_ref, x_ref, xn_ref, g_ref, w_ref, b_ref, cw_ref, cb_ref,
                  s_ref, x0_ref, z_ref, *, tl, nt):
    i = pl.program_id(1)
    xh = jnp.concatenate([xp_ref[...], x_ref[...], xn_ref[...]], axis=0)
    h = _rms(xh, g_ref[...]).astype(BF16)
    z = jnp.dot(h, w_ref[...], preferred_element_type=F32) + b_ref[...]
    row = lax.broadcasted_iota(jnp.int32, (tl + 16, 1), 0)
    inside = ((row >= 8) | (i > 0)) & ((row < tl + 8) | (i < nt - 1))
    z_ref[...] = jnp.where(inside, z, 0.0)

    def conv(c0):
        sl = slice(c0, c0 + D_MODEL)
        return (cw_ref[0:1, sl] * z_ref[7:7 + tl, sl] + cw_ref[1:2, sl] * z_ref[8:8 + tl, sl]
                + cw_ref[2:3, sl] * z_ref[9:9 + tl, sl] + cb_ref[:, sl])

    x0_ref[...] = conv(0)
    s_ref[...] = conv(2 * D_MODEL) * conv(D_MODEL)


def _hyena_in(x, g, w, b, cw, cb, tl=512):
    nt = SEQ // tl
    t8 = tl // 8
    kern = functools.partial(_hy_in_kernel, tl=tl, nt=nt)
    act = jax.ShapeDtypeStruct((BATCH, SEQ, D_MODEL), F32)
    return pl.pallas_call(
        kern, out_shape=(act, act), grid=(BATCH, nt),
        in_specs=[
            pl.BlockSpec((None, 8, D_MODEL), lambda bb, i: (bb, jnp.maximum(i * t8 - 1, 0), 0)),
            pl.BlockSpec((None, tl, D_MODEL), lambda bb, i: (bb, i, 0)),
            pl.BlockSpec((None, 8, D_MODEL), lambda bb, i: (bb, jnp.minimum((i + 1) * t8, SEQ // 8 - 1), 0)),
            _const_spec((1, D_MODEL)), _const_spec((D_MODEL, 3 * D_MODEL)), _const_spec((1, 3 * D_MODEL)),
            _const_spec((3, 3 * D_MODEL)), _const_spec((1, 3 * D_MODEL)),
        ],
        out_specs=(pl.BlockSpec((None, tl, D_MODEL), lambda bb, i: (bb, i, 0)),
                   pl.BlockSpec((None, tl, D_MODEL), lambda bb, i: (bb, i, 0))),
        scratch_shapes=[pltpu.VMEM((tl + 16, 3 * D_MODEL), F32)],
        compiler_params=_cparams(("parallel", "parallel")), name="hyena_in",
    )(x, x, x, g, w, b, cw, cb)


def _filter_kernel(feat_ref, w1_ref, b1_ref, w2_ref, b2_ref, w3_ref, b3_ref, fr_ref, wo_ref, dec_ref,
                   g_ref, sum_ref):
    hp = lax.Precision.HIGHEST
    feat = feat_ref[...]
    fr = fr_ref[...]
    h = jnp.sin(fr * (jnp.dot(feat, w1_ref[...], precision=hp, preferred_element_type=F32) + b1_ref[...]))
    h = jnp.sin(fr * (jnp.dot(h, w2_ref[...], precision=hp, preferred_element_type=F32) + b2_ref[...]))
    h = jnp.sin(fr * (jnp.dot(h, w3_ref[...], precision=hp, preferred_element_type=F32) + b3_ref[...]))
    o = jnp.dot(h, wo_ref[...], precision=hp, preferred_element_type=F32)
    t = feat[:, 0:1]
    keep = feat[:, HY_EMB_DIM:HY_EMB_DIM + 1]
    gq = o * (jnp.exp(-t * jnp.abs(dec_ref[...])) * keep)
    g_ref[...] = gq.astype(BF16)

    @pl.when(pl.program_id(0) == 0)
    def _():
        sum_ref[...] = jnp.zeros_like(sum_ref)

    sum_ref[...] += jnp.sum(jnp.abs(gq), axis=0, keepdims=True)


def _hyena_filter(feat, w1, b1, w2, b2, w3, b3, freq, wout, decay, tr=1024):
    nt = 2 * SEQ // tr
    half = nt // 2
    return pl.pallas_call(
        _filter_kernel,
        out_shape=(jax.ShapeDtypeStruct((2 * SEQ, D_MODEL), BF16), jax.ShapeDtypeStruct((1, D_MODEL), F32)),
        grid=(nt,),
        in_specs=[
            pl.BlockSpec((tr, HY_EMB_PAD), lambda i: (i, 0)),
            _const_spec((HY_EMB_PAD, HY_WIDTH)), _const_spec((1, HY_WIDTH)),
            _const_spec((HY_WIDTH, HY_WIDTH)), _const_spec((1, HY_WIDTH)),
            _const_spec((HY_WIDTH, HY_WIDTH)), _const_spec((1, HY_WIDTH)),
            _const_spec((1, HY_WIDTH)),
            pl.BlockSpec((HY_WIDTH, D_MODEL), lambda i: (0, i // half)),
            pl.BlockSpec((None, 1, D_MODEL), lambda i: (i // half, 0, 0)),
        ],
        out_specs=(pl.BlockSpec((tr, D_MODEL), lambda i: (i, 0)),
                   pl.BlockSpec((1, D_MODEL), lambda i: (0, 0))),
        compiler_params=_cparams(("arbitrary",)), name="hyena_filter",
    )(feat, w1, b1, w2, b2, w3, b3, freq, wout, decay)


def _select_column(tab_ref, idx):
    tab = tab_ref[...]
    lane = lax.broadcasted_iota(jnp.int32, tab.shape, 1)
    return jnp.sum(jnp.where(lane == idx, tab, 0.0), axis=-1, keepdims=True)


def _fft1_kernel(s_ref, m_ref, twc_ref, tws_ref, a_ref):
    n2 = pl.program_id(0)
    rhs = jnp.concatenate([s_ref[0], s_ref[1]], axis=0).astype(BF16)
    a = jnp.dot(m_ref[...], rhs, preferred_element_type=F32)
    ar, ai = a[:FFT_N1], a[FFT_N1:]
    c = _select_column(twc_ref, n2)
    s = _select_column(tws_ref, n2)
    a_ref[0] = (ar * c - ai * s).astype(BF16)
    a_ref[1] = (ar * s + ai * c).astype(BF16)


def _fft_stage1(sv, m, twc, tws):
    return pl.pallas_call(
        _fft1_kernel, out_shape=jax.ShapeDtypeStruct((2, FFT_N1, FFT_N2 * D_MODEL), BF16),
        grid=(FFT_N2,),
        in_specs=[pl.BlockSpec((2, FFT_H1, D_MODEL), lambda j: (0, 0, j)),
                  _const_spec((2 * FFT_N1, FFT_N1)),
                  _const_spec((FFT_N1, FFT_N2)), _const_spec((FFT_N1, FFT_N2))],
        out_specs=pl.BlockSpec((2, FFT_N1, D_MODEL), lambda j: (0, 0, j)),
        compiler_params=_cparams(("parallel",)), name="fft_stage1",
    )(sv, m, twc, tws)


def _fft2_filter_kernel(a_ref, m_ref, sum_ref, g_ref, *, kb):
    scale = 1.0 / (sum_ref[...] * FFT_N)
    for q in range(kb):
        rhs = jnp.concatenate([a_ref[0, q], a_ref[1, q]], axis=0)
        z = jnp.dot(m_ref[...], rhs, preferred_element_type=F32)
        g_ref[0, q] = z[:FFT_N2] * scale
        g_ref[1, q] = z[FFT_N2:] * scale


def _fft_stage2_filter(a, m2, gsum, kb=4):
    kern = functools.partial(_fft2_filter_kernel, kb=kb)
    blk = pl.BlockSpec((2, kb, FFT_N2, D_MODEL), lambda k: (0, k, 0, 0))
    return pl.pallas_call(
        kern, out_shape=jax.ShapeDtypeStruct((2, FFT_N1, FFT_N2, D_MODEL), F32),
        grid=(FFT_N1 // kb,),
        in_specs=[blk, _const_spec((2 * FFT_N2, 2 * FFT_N2)), _const_spec((1, D_MODEL))],
        out_specs=blk,
        compiler_params=_cparams(("parallel",)), name="fft_stage2_filter",
    )(a, m2, gsum)


def _fft2_kernel(a_ref, g_ref, m_ref, mc_ref, twc_ref, tws_ref, b_ref, *, kb):
    k0 = pl.program_id(0) * kb
    for q in range(kb):
        rhs = jnp.concatenate([a_ref[0, q], a_ref[1, q]], axis=0)
        z = jnp.dot(m_ref[...], rhs, preferred_element_type=F32)
        zr, zi = z[:FFT_N2], z[FFT_N2:]
        gr, gi = g_ref[0, q], g_ref[1, q]
        y = jnp.concatenate([zr * gr - zi * gi, zr * gi + zi * gr], axis=0).astype(BF16)
        bb = jnp.dot(mc_ref[...], y, preferred_element_type=F32)
        br, bi = bb[:FFT_N2], bb[FFT_N2:]
        c = _select_column(twc_ref, k0 + q)
        s = _select_column(tws_ref, k0 + q)
        b_ref[0, q] = (br * c + bi * s).astype(BF16)
        b_ref[1, q] = (bi * c - br * s).astype(BF16)


def _fft_stage2(a, g, m2, m2c, twc_t, tws_t, kb=4):
    kern = functools.partial(_fft2_kernel, kb=kb)
    blk = pl.BlockSpec((2, kb, FFT_N2, D_MODEL), lambda k: (0, k, 0, 0))
    return pl.pallas_call(
        kern, out_shape=jax.ShapeDtypeStruct((2, FFT_N1, FFT_N2, D_MODEL), BF16),
        grid=(FFT_N1 // kb,),
        in_specs=[blk, blk, _const_spec((2 * FFT_N2, 2 * FFT_N2)), _const_spec((2 * FFT_N2, 2 * FFT_N2)),
                  _const_spec((FFT_N2, FFT_N1)), _const_spec((FFT_N2, FFT_N1))],
        out_specs=blk,
        compiler_params=_cparams(("parallel",)), name="fft_stage2",
    )(a, g, m2, m2c, twc_t, tws_t)


def _fft3_kernel(b_ref, s_ref, x0_ref, skip_ref, m_ref, y_ref):
    rhs = jnp.concatenate([b_ref[0], b_ref[1]], axis=0)
    o = jnp.dot(m_ref[...], rhs, preferred_element_type=F32)
    skip = skip_ref[...]
    for bb in range(BATCH):
        y = o[bb * FFT_H1:(bb + 1) * FFT_H1]
        y_ref[bb] = ((y + s_ref[bb] * skip) * x0_ref[bb]).astype(BF16)


def _fft_stage3(bv, sv, x0v, skip, m3):
    act = pl.BlockSpec((2, FFT_H1, D_MODEL), lambda j: (0, 0, j))
    return pl.pallas_call(
        _fft3_kernel, out_shape=jax.ShapeDtypeStruct((BATCH, FFT_H1, FFT_N2 * D_MODEL), BF16),
        grid=(FFT_N2,),
        in_specs=[pl.BlockSpec((2, FFT_N1, D_MODEL), lambda j: (0, 0, j)), act, act,
                  _const_spec((1, D_MODEL)), _const_spec((FFT_N1, 2 * FFT_N1))],
        out_specs=act,
        compiler_params=_cparams(("parallel",)), name="fft_stage3",
    )(bv, sv, x0v, skip, m3)


def _out_proj_kernel(x_ref, y_ref, w_ref, b_ref, o_ref):
    o_ref[...] = x_ref[...] + jnp.dot(y_ref[...], w_ref[...], preferred_element_type=F32) + b_ref[...]


def _out_proj(x, y, w, b, tm=1024):
    m = x.shape[0]
    return pl.pallas_call(
        _out_proj_kernel, out_shape=jax.ShapeDtypeStruct((m, D_MODEL), F32), grid=(m // tm,),
        in_specs=[pl.BlockSpec((tm, D_MODEL), lambda i: (i, 0)), pl.BlockSpec((tm, D_MODEL), lambda i: (i, 0)),
                  _const_spec((D_MODEL, D_MODEL)), _const_spec((1, D_MODEL))],
        out_specs=pl.BlockSpec((tm, D_MODEL), lambda i: (i, 0)),
        compiler_params=_cparams(("parallel",)), name="out_proj",
    )(x, y, w, b)


def _ffn_kernel(x_ref, g_ref, wg_ref, wu_ref, wd_ref, gf_ref, o_ref, *, final):
    x = x_ref[...]
    h = _rms(x, g_ref[...]).astype(BF16)
    gate = jnp.dot(h, wg_ref[...], preferred_element_type=F32)
    up = jnp.dot(h, wu_ref[...], preferred_element_type=F32)
    a = (gate * jax.nn.sigmoid(gate) * up).astype(BF16)
    y = x + jnp.dot(a, wd_ref[...], preferred_element_type=F32)
    if final:
        y = _rms(y, gf_ref[...])
    o_ref[...] = y


def _ffn(x, g, wg, wu, wd, gf, final, tm=512):
    m = x.shape[0]
    kern = functools.partial(_ffn_kernel, final=final)
    return pl.pallas_call(
        kern, out_shape=jax.ShapeDtypeStruct((m, D_MODEL), F32), grid=(m // tm,),
        in_specs=[pl.BlockSpec((tm, D_MODEL), lambda i: (i, 0)), _const_spec((1, D_MODEL)),
                  _const_spec((D_MODEL, FFN_HIDDEN)), _const_spec((D_MODEL, FFN_HIDDEN)),
                  _const_spec((FFN_HIDDEN, D_MODEL)), _const_spec((1, D_MODEL))],
        out_specs=pl.BlockSpec((tm, D_MODEL), lambda i: (i, 0)),
        compiler_params=_cparams(("parallel",)), name="ffn",
    )(x, g, wg, wu, wd, gf)


def _qkv_kernel(x_ref, g_ref, w_ref, b_ref, q_ref, k_ref, v_ref):
    h = _rms(x_ref[...], g_ref[...]).astype(BF16)
    z = jnp.dot(h, w_ref[...], preferred_element_type=F32) + b_ref[...]
    q_ref[...] = (z[:, :D_MODEL] * (NA_HEAD_DIM ** -0.5)).astype(BF16)
    k_ref[...] = z[:, D_MODEL:2 * D_MODEL].astype(BF16)
    v_ref[...] = z[:, 2 * D_MODEL:].astype(BF16)


def _qkv(x, g, w, b, tm=512):
    m = x.shape[0]
    act = jax.ShapeDtypeStruct((m, D_MODEL), BF16)
    blk = pl.BlockSpec((tm, D_MODEL), lambda i: (i, 0))
    return pl.pallas_call(
        _qkv_kernel, out_shape=(act, act, act), grid=(m // tm,),
        in_specs=[blk, _const_spec((1, D_MODEL)), _const_spec((D_MODEL, 3 * D_MODEL)),
                  _const_spec((1, 3 * D_MODEL))],
        out_specs=(blk, blk, blk),
        compiler_params=_cparams(("parallel",)), name="na_qkv",
    )(x, g, w, b)


def _na_kernel(q_ref, k0_ref, k1_ref, k2_ref, v0_ref, v1_ref, v2_ref, bias_ref, o_ref):
    lane = lax.broadcasted_iota(jnp.int32, (1, 128), 1)
    halves = (lane < NA_HEAD_DIM, lane >= NA_HEAD_DIM)
    for p in range(NA_HEADS // 2):
        sl = slice(128 * p, 128 * (p + 1))
        qp = q_ref[:, sl]
        kp = jnp.concatenate([k0_ref[:, sl], k1_ref[:, sl], k2_ref[:, sl]], axis=0)
        vp = jnp.concatenate([v0_ref[:, sl], v1_ref[:, sl], v2_ref[:, sl]], axis=0)
        acc = jnp.zeros((NA_QR * GRID_W, 128), F32)
        for hh in range(2):
            qm = jnp.where(halves[hh], qp, jnp.zeros_like(qp))
            s = lax.dot_general(qm, kp, (((1,), (1,)), ((), ())), preferred_element_type=F32)
            s = s + bias_ref[2 * p + hh]
            e = jnp.exp(s - jnp.max(s, axis=-1, keepdims=True))
            inv = 1.0 / jnp.sum(e, axis=-1, keepdims=True)
            vm = jnp.where(halves[hh], vp, jnp.zeros_like(vp))
            acc = acc + jnp.dot(e.astype(BF16), vm, preferred_element_type=F32) * inv
        o_ref[:, sl] = acc.astype(BF16)


def _na_attention(q, k, v, bias):
    tok = NA_QR * GRID_W

    def chunk(t):
        return pl.BlockSpec((None, None, tok, D_MODEL),
                            lambda bb, j: (bb, jnp.clip(j - 1, 0, NA_NBLK - 3) + t, 0, 0))

    def btype(bb, j):
        return (jnp.where(j == 0, 0, jnp.where(j == NA_NBLK - 1, 2, 1)), 0, 0, 0)

    qspec = pl.BlockSpec((None, None, tok, D_MODEL), lambda bb, j: (bb, j, 0, 0))
    return pl.pallas_call(
        _na_kernel, out_shape=jax.ShapeDtypeStruct(q.shape, BF16), grid=(BATCH, NA_NBLK),
        in_specs=[qspec, chunk(0), chunk(1), chunk(2), chunk(0), chunk(1), chunk(2),
                  pl.BlockSpec((None, NA_HEADS, tok, NA_KR * GRID_W), btype, pipeline_mode=pl.Buffered(1))],
        out_specs=qspec,
        compiler_params=_cparams(("parallel", "arbitrary")), name="na_attention",
    )(q, k, k, k, v, v, v, bias)


def kernel(x, norm_mix, norm_ffn, norm_final, hy_w_in, hy_b_in, hy_conv_w, hy_conv_b, hy_f_w1, hy_f_b1, hy_f_w2, hy_f_b2, hy_f_w3, hy_f_b3, hy_f_freq, hy_f_wout, hy_decay, hy_skip, hy_w_out, hy_b_out, na_w_qkv, na_b_qkv, na_rpb, na_w_o, na_b_o, ffn_w_gate, ffn_w_up, ffn_w_down):
    tabs = _fft_tables()
    row = lambda a: a.reshape(1, -1)
    m_tok = BATCH * SEQ

    s, x0 = _hyena_in(x, row(norm_mix[0]), hy_w_in[0].astype(BF16), row(hy_b_in[0]),
                      hy_conv_w[0], row(hy_conv_b[0]))
    w1 = jnp.zeros((HY_EMB_PAD, HY_WIDTH), F32).at[:HY_EMB_DIM].set(hy_f_w1[0])
    g_circ, g_sum = _hyena_filter(_filter_features(), w1, row(hy_f_b1[0]), hy_f_w2[0], row(hy_f_b2[0]),
                                  hy_f_w3[0], row(hy_f_b3[0]), row(hy_f_freq[0]), hy_f_wout[0],
                                  hy_decay[0].reshape(2, 1, D_MODEL))
    view = lambda a: a.reshape(2, FFT_H1, FFT_N2 * D_MODEL)
    ga = _fft_stage1(view(g_circ), tabs["m1g"], tabs["twc"], tabs["tws"])
    gspec = _fft_stage2_filter(ga.reshape(2, FFT_N1, FFT_N2, D_MODEL), tabs["m2"], g_sum)
    sa = _fft_stage1(view(s), tabs["m1"], tabs["twc"], tabs["tws"])
    sb = _fft_stage2(sa.reshape(2, FFT_N1, FFT_N2, D_MODEL), gspec, tabs["m2"], tabs["m2c"],
                     tabs["twc_t"], tabs["tws_t"])
    yv = _fft_stage3(sb.reshape(2, FFT_N1, FFT_N2 * D_MODEL), view(s), view(x0), row(hy_skip[0]), tabs["m3"])
    xf = _out_proj(x.reshape(m_tok, D_MODEL), yv.reshape(m_tok, D_MODEL), hy_w_out[0].astype(BF16),
                   row(hy_b_out[0]))
    xf = _ffn(xf, row(norm_ffn[0]), ffn_w_gate[0].astype(BF16), ffn_w_up[0].astype(BF16),
              ffn_w_down[0].astype(BF16), row(norm_final), final=False)

    q, k, v = _qkv(xf, row(norm_mix[1]), na_w_qkv[0].astype(BF16), row(na_b_qkv[0]))
    ro, co, ok = _na_bias_index()
    bias = jnp.where(ok[None], na_rpb[0][:, ro, co], NEG_BIG)
    bias = jnp.transpose(bias, (1, 0, 2, 3, 4, 5)).reshape(3, NA_HEADS, NA_QR * GRID_W, NA_KR * GRID_W)
    blk = lambda a: a.reshape(BATCH, NA_NBLK, NA_QR * GRID_W, D_MODEL)
    att = _na_attention(blk(q), blk(k), blk(v), bias)
    xf = _out_proj(xf, att.reshape(m_tok, D_MODEL), na_w_o[0].astype(BF16), row(na_b_o[0]))
    xf = _ffn(xf, row(norm_ffn[1]), ffn_w_gate[1].astype(BF16), ffn_w_up[1].astype(BF16),
              ffn_w_down[1].astype(BF16), row(norm_final), final=True)
    return xf.reshape(BATCH, SEQ, D_MODEL)
```

```python
import functools
import math

import numpy as np
import jax
import jax.numpy as jnp
from jax import lax
from jax.experimental import pallas as pl
from jax.experimental.pallas import tpu as pltpu

F32 = jnp.float32
BF16 = jnp.bfloat16

D_MODEL = 1024
SEQ = 16384
BATCH = 2
FFN_HIDDEN = 2816
GRID_W = 64
RMS_EPS = 1e-6
HY_EMB_DIM = 33
HY_EMB_PAD = 40
HY_N_BANDS = 16
HY_WIDTH = 64
NA_HEADS = 16
NA_HEAD_DIM = 64
NA_KH = 8
NA_KW = 16

FFT_N = 2 * SEQ
FFT_N1 = 256
FFT_N2 = 128
FFT_H1 = FFT_N1 // 2
FFT_NB = 8
FFT_LANES = 512

NA_ROWS = SEQ // GRID_W
NA_QR = 4
NA_KR = 12
NA_NBLK = NA_ROWS // NA_QR
NEG_BIG = -1e30
LOG2_E = math.log2(math.e)

VMEM_LIMIT = 56 * 1024 * 1024


def _cparams(sem):
    return pltpu.CompilerParams(dimension_semantics=sem, vmem_limit_bytes=VMEM_LIMIT)


def _const_spec(shape):
    nd = len(shape)
    return pl.BlockSpec(shape, lambda *_: (0,) * nd, pipeline_mode=pl.Buffered(1))


def _rms(x, g):
    ms = jnp.mean(x * x, axis=-1, keepdims=True)
    return x * lax.rsqrt(ms + RMS_EPS) * g


@functools.lru_cache(maxsize=None)
def _fft_tables():
    n1, n2, n = FFT_N1, FFT_N2, FFT_N
    k1 = np.arange(n1)[:, None]
    nh = np.arange(FFT_H1)[None, :]
    nf = np.arange(n1)[None, :]
    c = np.exp(-2j * np.pi * (nh * k1) / n1)
    cf = np.exp(-2j * np.pi * (nf * k1) / n1)
    e = np.exp(2j * np.pi * (nh * k1) / n1).T
    m3 = np.block([[e.real, -e.imag], [e.imag, e.real]])
    m3 = m3.reshape(2, FFT_H1, 2 * n1)
    a = np.arange(n2)
    f = np.exp(-2j * np.pi * np.outer(a, a) / n2)
    m2 = np.block([[f.real, -f.imag], [f.imag, f.real]])
    tw = np.exp(-2j * np.pi * np.outer(np.arange(n1), np.arange(n2)) / n)
    as_f = lambda m: np.asarray(m, dtype=np.float32)
    return dict(cr=as_f(c.real), ci=as_f(c.imag), cfr=as_f(cf.real), cfi=as_f(cf.imag), m3=as_f(m3),
                m2=as_f(m2), fr=as_f(f.real), fi=as_f(f.imag),
                twc=as_f(tw.real), tws=as_f(tw.imag),
                twc_t=as_f(tw.real.T), tws_t=as_f(tw.imag.T))


@functools.lru_cache(maxsize=None)
def _filter_features():
    ell = SEQ
    t = np.linspace(0.0, 1.0, ell, dtype=np.float32).astype(np.float64)
    w = (2.0 * math.pi * np.arange(ell, dtype=np.float32) / ell).astype(np.float32).astype(np.float64)
    bands = np.linspace(1e-4, HY_N_BANDS - 1, HY_N_BANDS, dtype=np.float32).astype(np.float64)
    ang = (bands[None, :] * w[:, None]).astype(np.float32).astype(np.float64)
    feat = np.concatenate([t[:, None], np.cos(ang), -np.sin(ang)], axis=-1)
    tab = np.zeros((2 * ell, HY_EMB_PAD), np.float64)
    tab[:ell, :HY_EMB_DIM] = feat
    tab[:ell, HY_EMB_DIM] = 1.0
    tab[ell + 1:, :HY_EMB_DIM] = feat[:0:-1]
    tab[ell + 1:, HY_EMB_DIM] = 1.0
    j, d, i, h = np.meshgrid(np.arange(FFT_N2 // FFT_NB), np.arange(2), np.arange(FFT_NB), np.arange(FFT_H1),
                             indexing="ij")
    perm = (FFT_N2 * (d * FFT_H1 + h) + FFT_NB * j + i).reshape(-1)
    tab = tab[perm]
    return np.asarray(tab.T, dtype=np.float32), np.asarray(tab[:, 0:1], dtype=np.float32)


def _na_bias_plan():
    plan = []
    for j in (0, 1, NA_NBLK - 1):
        base = NA_QR * min(max(j - 1, 0), NA_NBLK - 3)
        per_type = []
        for ri in range(NA_QR):
            r = NA_QR * j + ri
            rs = min(max(r - NA_KH // 2, 0), NA_ROWS - NA_KH)
            row = []
            for m in range(NA_KR // 2):
                kr = base + 2 * m
                row.append((kr - r + (NA_KH - 1), 0 <= kr - rs < NA_KH, 0 <= kr + 1 - rs < NA_KH))
            per_type.append(row)
        plan.append(per_type)
    return plan


def _na_bias_kernel(v_ref, o_ref):
    n_e = 2 * NA_KH
    c = lax.broadcasted_iota(jnp.int32, (GRID_W, 128), 0)
    lane = lax.broadcasted_iota(jnp.int32, (GRID_W, 128), 1)
    kc = lane % GRID_W
    cs = jnp.clip(c - NA_KW // 2, 0, GRID_W - NA_KW)
    col_ok = (kc >= cs) & (kc < cs + NA_KW)
    left = lane < GRID_W
    tiles = []
    for e in range(n_e):
        x = jnp.broadcast_to(v_ref[e:e + 1, :] * LOG2_E, (GRID_W, 128))
        tiles.append(pltpu.roll(x, 128 - (NA_KW - 1), 1, stride=1, stride_axis=0))
    neg = jnp.full((GRID_W, 128), NEG_BIG, F32)
    for t, per_type in enumerate(_na_bias_plan()):
        for ri, row in enumerate(per_type):
            for m, (ro_left, ok_l, ok_r) in enumerate(row):
                if not (ok_l or ok_r):
                    tile = neg
                else:
                    mask = col_ok if (ok_l and ok_r) else (col_ok & left if ok_l else col_ok & ~left)
                    tile = jnp.where(mask, tiles[ro_left + 1], NEG_BIG)
                o_ref[t, ri * GRID_W:(ri + 1) * GRID_W, m * 128:(m + 1) * 128] = tile


def _na_bias(rpb):
    n_e = 2 * NA_KH
    rp = jnp.pad(rpb, ((0, 0), (1, 1), (0, GRID_W - (2 * NA_KW - 1))))
    v = jnp.concatenate([rp[:, :n_e], rp[:, 1:n_e + 1]], axis=-1)
    tok, keys = NA_QR * GRID_W, NA_KR * GRID_W
    return pl.pallas_call(
        _na_bias_kernel, out_shape=jax.ShapeDtypeStruct((3, NA_HEADS, tok, keys), F32), grid=(NA_HEADS,),
        in_specs=[pl.BlockSpec((None, n_e, 128), lambda h: (h, 0, 0))],
        out_specs=pl.BlockSpec((3, None, tok, keys), lambda h: (0, h, 0, 0)),
        compiler_params=_cparams(("parallel",)), name="na_bias",
    )(v)


def _hy_in_kernel(xp_ref, x_ref, xn_ref, g_ref, w_ref, b_ref, cw_ref, cb_ref,
                  s_ref, x0_ref, z_ref, *, tl, nt):
    i = pl.program_id(1)
    xh = jnp.concatenate([xp_ref[...], x_ref[...], xn_ref[...]], axis=0)
    h = _rms(xh, g_ref[...]).astype(BF16)
    z = jnp.dot(h, w_ref[...], preferred_element_type=F32) + b_ref[...]
    row = lax.broadcasted_iota(jnp.int32, (tl + 16, 1), 0)
    inside = ((row >= 8) | (i > 0)) & ((row < tl + 8) | (i < nt - 1))
    z = jnp.where(inside, z, 0.0)
    z_ref[0] = pltpu.roll(z, 1, 0)
    z_ref[1] = z
    z_ref[2] = pltpu.roll(z, tl + 15, 0)

    def conv(c0):
        sl = slice(c0, c0 + D_MODEL)
        rows = slice(8, 8 + tl)
        return (cw_ref[0:1, sl] * z_ref[0, rows, sl] + cw_ref[1:2, sl] * z_ref[1, rows, sl]
                + cw_ref[2:3, sl] * z_ref[2, rows, sl] + cb_ref[:, sl])

    x0_ref[...] = conv(0)
    s_ref[...] = conv(2 * D_MODEL) * conv(D_MODEL)


def _hyena_in(x, g, w, b, cw, cb, tl=512):
    nt = SEQ // tl
    t8 = tl // 8
    kern = functools.partial(_hy_in_kernel, tl=tl, nt=nt)
    act = jax.ShapeDtypeStruct((BATCH, SEQ, D_MODEL), F32)
    return pl.pallas_call(
        kern, out_shape=(act, act), grid=(BATCH, nt),
        in_specs=[
            pl.BlockSpec((None, 8, D_MODEL), lambda bb, i: (bb, jnp.maximum(i * t8 - 1, 0), 0)),
            pl.BlockSpec((None, tl, D_MODEL), lambda bb, i: (bb, i, 0)),
            pl.BlockSpec((None, 8, D_MODEL), lambda bb, i: (bb, jnp.minimum((i + 1) * t8, SEQ // 8 - 1), 0)),
            _const_spec((1, D_MODEL)), _const_spec((D_MODEL, 3 * D_MODEL)), _const_spec((1, 3 * D_MODEL)),
            _const_spec((3, 3 * D_MODEL)), _const_spec((1, 3 * D_MODEL)),
        ],
        out_specs=(pl.BlockSpec((None, tl, D_MODEL), lambda bb, i: (bb, i, 0)),
                   pl.BlockSpec((None, tl, D_MODEL), lambda bb, i: (bb, i, 0))),
        scratch_shapes=[pltpu.VMEM((3, tl + 16, 3 * D_MODEL), F32)],
        compiler_params=_cparams(("parallel", "parallel")), name="hyena_in",
    )(x, x, x, g, w, b, cw, cb)


def _dot3(a, b, dims):
    a_hi = a.astype(BF16)
    a_lo = (a - a_hi.astype(F32)).astype(BF16)
    b_hi = b.astype(BF16)
    b_lo = (b - b_hi.astype(F32)).astype(BF16)
    dot = lambda u, v: lax.dot_general(u, v, dims, preferred_element_type=F32)
    return dot(a_hi, b_hi) + dot(a_hi, b_lo) + dot(a_lo, b_hi)


def _filter_kernel(ft_ref, tc_ref, w1_ref, b1_ref, w2_ref, b2_ref, w3_ref, b3_ref, fr_ref, wo_ref, dec_ref,
                   g_ref, sum_ref):
    hp = lax.Precision.HIGHEST
    ft = ft_ref[...]
    fr = fr_ref[...]
    h = jnp.sin(fr * (jnp.dot(w1_ref[...], ft, precision=hp, preferred_element_type=F32) + b1_ref[...]))
    h = jnp.sin(fr * (jnp.dot(w2_ref[...], h, precision=hp, preferred_element_type=F32) + b2_ref[...]))
    h = jnp.sin(fr * (jnp.dot(w3_ref[...], h, precision=hp, preferred_element_type=F32) + b3_ref[...]))
    h = h * ft[HY_EMB_DIM:HY_EMB_DIM + 1, :]
    half = FFT_NB * FFT_H1
    total = jnp.zeros((1, D_MODEL), F32)
    for d in range(2):
        hd = h[:, d * half:(d + 1) * half]
        o = _dot3(hd, wo_ref[:, d * D_MODEL:(d + 1) * D_MODEL], (((0,), (0,)), ((), ())))
        t = tc_ref[d * half:(d + 1) * half, :]
        gq = o * jnp.exp(-t * jnp.abs(dec_ref[d:d + 1, :]))
        total = total + jnp.sum(jnp.abs(gq), axis=0, keepdims=True)
        for i in range(FFT_NB):
            g_ref[i, d * FFT_H1:(d + 1) * FFT_H1, :] = gq[i * FFT_H1:(i + 1) * FFT_H1].astype(BF16)

    @pl.when(pl.program_id(0) == 0)
    def _():
        sum_ref[...] = jnp.zeros_like(sum_ref)

    sum_ref[...] += total


def _hyena_filter(feat_t, t_col, w1t, b1, w2t, b2, w3t, b3, freq, wout, decay):
    tr = 2 * FFT_NB * FFT_H1
    col = lambda: _const_spec((HY_WIDTH, 1))
    return pl.pallas_call(
        _filter_kernel,
        out_shape=(jax.ShapeDtypeStruct((FFT_N2, FFT_N1, D_MODEL), BF16),
                   jax.ShapeDtypeStruct((1, D_MODEL), F32)),
        grid=(FFT_N2 // FFT_NB,),
        in_specs=[
            pl.BlockSpec((HY_EMB_PAD, tr), lambda j: (0, j)),
            pl.BlockSpec((tr, 1), lambda j: (j, 0)),
            _const_spec((HY_WIDTH, HY_EMB_PAD)), col(),
            _const_spec((HY_WIDTH, HY_WIDTH)), col(),
            _const_spec((HY_WIDTH, HY_WIDTH)), col(),
            col(),
            _const_spec((HY_WIDTH, 2 * D_MODEL)),
            _const_spec((2, D_MODEL)),
        ],
        out_specs=(pl.BlockSpec((FFT_NB, FFT_N1, D_MODEL), lambda j: (j, 0, 0)),
                   pl.BlockSpec((1, D_MODEL), lambda j: (0, 0))),
        compiler_params=_cparams(("arbitrary",)), name="hyena_filter",
    )(feat_t, t_col, w1t, b1, w2t, b2, w3t, b3, freq, wout, decay)


def _select_column(tab_ref, idx):
    tab = tab_ref[...]
    lane = lax.broadcasted_iota(jnp.int32, tab.shape, 1)
    return jnp.sum(jnp.where(lane == idx, tab, 0.0), axis=-1, keepdims=True)


def _stack_bf16(re, im):
    return jnp.concatenate([re, im], axis=0).astype(BF16)


def _twiddled(cr, ci, c, s):
    return cr * c - ci * s, cr * s + ci * c


def _fft1_kernel(s_ref, cr_ref, ci_ref, twc_ref, tws_ref, ar_ref, ai_ref, t_ref):
    n2_0 = pl.program_id(1) * FFT_NB
    for i in range(FFT_NB):
        pr, pi = _twiddled(cr_ref[...], ci_ref[...],
                           _select_column(twc_ref, n2_0 + i), _select_column(tws_ref, n2_0 + i))
        m = jnp.concatenate([jnp.concatenate([pr, -pi], axis=1), jnp.concatenate([pi, pr], axis=1)], axis=0)
        slot = i % 2
        t_ref[slot, :FFT_H1] = s_ref[0, :, i, :]
        t_ref[slot, FFT_H1:] = s_ref[1, :, i, :]
        a = jnp.dot(m.astype(BF16), t_ref[slot].astype(BF16), preferred_element_type=F32)
        ar_ref[:, i, :] = a[:FFT_N1]
        ai_ref[:, i, :] = a[FFT_N1:]


def _fft1_filter_kernel(g_ref, cr_ref, ci_ref, twc_ref, tws_ref, ar_ref, ai_ref):
    n2_0 = pl.program_id(1) * FFT_NB
    for i in range(FFT_NB):
        pr, pi = _twiddled(cr_ref[...], ci_ref[...],
                           _select_column(twc_ref, n2_0 + i), _select_column(tws_ref, n2_0 + i))
        m = jnp.concatenate([pr, pi], axis=0).astype(BF16)
        a = jnp.dot(m, g_ref[i], preferred_element_type=F32)
        ar_ref[:, i, :] = a[:FFT_N1]
        ai_ref[:, i, :] = a[FFT_N1:]


def _fft_stage1(s4, g3, cr, ci, twc, tws):
    consts = [_const_spec(cr.shape), _const_spec(ci.shape),
              _const_spec((FFT_N1, FFT_N2)), _const_spec((FFT_N1, FFT_N2))]
    out = jax.ShapeDtypeStruct((FFT_N1, FFT_N2, D_MODEL), F32)
    oblk = pl.BlockSpec((FFT_N1, FFT_NB, FFT_LANES), lambda h, j: (0, j, h))
    common = dict(out_shape=(out, out), grid=(D_MODEL // FFT_LANES, FFT_N2 // FFT_NB), out_specs=(oblk, oblk),
                  compiler_params=_cparams(("parallel", "parallel")))
    if g3 is None:
        return pl.pallas_call(
            _fft1_kernel,
            in_specs=[pl.BlockSpec((2, FFT_H1, FFT_NB, FFT_LANES), lambda h, j: (0, 0, j, h))] + consts,
            scratch_shapes=[pltpu.VMEM((2, FFT_N1, FFT_LANES), F32)], name="fft_stage1", **common,
        )(s4, cr, ci, twc, tws)
    return pl.pallas_call(
        _fft1_filter_kernel,
        in_specs=[pl.BlockSpec((FFT_NB, FFT_N1, FFT_LANES), lambda h, j: (j, 0, h))] + consts,
        name="fft_stage1_filter", **common,
    )(g3, cr, ci, twc, tws)


def _fft2_kernel(ar_ref, ai_ref, gr_ref, gi_ref, m_ref, fr_ref, fi_ref, twc_ref, tws_ref, br_ref, bi_ref):
    k0 = pl.program_id(1) * FFT_NB
    m = m_ref[...]
    for q in range(FFT_NB):
        z = jnp.dot(m, _stack_bf16(ar_ref[q], ai_ref[q]), preferred_element_type=F32)
        g = jnp.dot(m, _stack_bf16(gr_ref[q], gi_ref[q]), preferred_element_type=F32)
        zr, zi = z[:FFT_N2], z[FFT_N2:]
        gr, gi = g[:FFT_N2], g[FFT_N2:]
        y = _stack_bf16(zr * gr - zi * gi, zr * gi + zi * gr)
        pr, pi = _twiddled(fr_ref[...], -fi_ref[...],
                           _select_column(twc_ref, k0 + q), -_select_column(tws_ref, k0 + q))
        mc = jnp.concatenate([jnp.concatenate([pr, -pi], axis=1), jnp.concatenate([pi, pr], axis=1)], axis=0)
        bb = jnp.dot(mc.astype(BF16), y, preferred_element_type=F32)
        br_ref[:, q, :] = bb[:FFT_N2]
        bi_ref[:, q, :] = bb[FFT_N2:]


def _fft_stage2(a, ag, m2, fr, fi, twc_t, tws_t):
    blk = pl.BlockSpec((FFT_NB, FFT_N2, FFT_LANES), lambda h, k: (k, 0, h))
    out = jax.ShapeDtypeStruct((FFT_N2, FFT_N1, D_MODEL), F32)
    oblk = pl.BlockSpec((FFT_N2, FFT_NB, FFT_LANES), lambda h, k: (0, k, h))
    return pl.pallas_call(
        _fft2_kernel, out_shape=(out, out), grid=(D_MODEL // FFT_LANES, FFT_N1 // FFT_NB),
        in_specs=[blk, blk, blk, blk, _const_spec((2 * FFT_N2, 2 * FFT_N2)),
                  _const_spec((FFT_N2, FFT_N2)), _const_spec((FFT_N2, FFT_N2)),
                  _const_spec((FFT_N2, FFT_N1)), _const_spec((FFT_N2, FFT_N1))],
        out_specs=(oblk, oblk),
        compiler_params=_cparams(("parallel", "parallel")), name="fft_stage2",
    )(*a, *ag, m2, fr, fi, twc_t, tws_t)


def _fft3_kernel(br_ref, bi_ref, s_ref, x0_ref, skip_ref, sum_ref, m_ref, y_ref, t_ref):
    skip = skip_ref[...]
    scale = 1.0 / (sum_ref[...] * FFT_N)
    for i in range(FFT_NB):
        y = jnp.dot(m_ref[...], _stack_bf16(br_ref[i], bi_ref[i]), preferred_element_type=F32)
        slot = i % 2
        t_ref[slot, 0] = s_ref[:, i, :]
        t_ref[slot, 1] = x0_ref[:, i, :]
        y_ref[i] = ((y * scale + t_ref[slot, 0] * skip) * t_ref[slot, 1]).astype(BF16)


def _fft_stage3(b, s4, x04, skip, gsum, m3):
    act = pl.BlockSpec((None, FFT_H1, FFT_NB, FFT_LANES), lambda h, j, bb: (bb, 0, j, h))
    bblk = pl.BlockSpec((FFT_NB, FFT_N1, FFT_LANES), lambda h, j, bb: (j, 0, h))
    vec = pl.BlockSpec((1, FFT_LANES), lambda h, j, bb: (0, h))
    return pl.pallas_call(
        _fft3_kernel, out_shape=jax.ShapeDtypeStruct((BATCH, FFT_N2, FFT_H1, D_MODEL), BF16),
        grid=(D_MODEL // FFT_LANES, FFT_N2 // FFT_NB, BATCH),
        in_specs=[bblk, bblk, act, act, vec, vec,
                  pl.BlockSpec((None, FFT_H1, 2 * FFT_N1), lambda h, j, bb: (bb, 0, 0))],
        out_specs=pl.BlockSpec((None, FFT_NB, FFT_H1, FFT_LANES), lambda h, j, bb: (bb, j, 0, h)),
        scratch_shapes=[pltpu.VMEM((2, 2, FFT_H1, FFT_LANES), F32)],
        compiler_params=_cparams(("parallel", "parallel", "arbitrary")), name="fft_stage3",
    )(*b, s4, x04, skip, gsum, m3)


def _out_proj_t_kernel(x_ref, y_ref, w_ref, b_ref, o_ref, t_ref):
    y = y_ref[...].reshape(FFT_NB * FFT_H1, D_MODEL)
    y = jnp.dot(y, w_ref[...], preferred_element_type=F32) + b_ref[...]
    for i in range(FFT_NB):
        slot = i % 2
        t_ref[slot] = x_ref[:, i, :]
        o_ref[:, i, :] = t_ref[slot] + y[i * FFT_H1:(i + 1) * FFT_H1]


def _out_proj_t(x4, y4, w, b):
    nat = pl.BlockSpec((None, FFT_H1, FFT_NB, D_MODEL), lambda bb, j: (bb, 0, j, 0))
    return pl.pallas_call(
        _out_proj_t_kernel, out_shape=jax.ShapeDtypeStruct(x4.shape, F32), grid=(BATCH, FFT_N2 // FFT_NB),
        in_specs=[nat, pl.BlockSpec((None, FFT_NB, FFT_H1, D_MODEL), lambda bb, j: (bb, j, 0, 0)),
                  _const_spec((D_MODEL, D_MODEL)), _const_spec((1, D_MODEL))],
        out_specs=nat,
        scratch_shapes=[pltpu.VMEM((2, FFT_H1, D_MODEL), F32)],
        compiler_params=_cparams(("parallel", "parallel")), name="out_proj_t",
    )(x4, y4, w, b)


def _swiglu_residual(x, g, wg_ref, wu_ref, wd_ref):
    h = _rms(x, g).astype(BF16)
    gate = jnp.dot(h, wg_ref[...], preferred_element_type=F32)
    up = jnp.dot(h, wu_ref[...], preferred_element_type=F32)
    a = (gate * jax.nn.sigmoid(gate) * up).astype(BF16)
    return x + jnp.dot(a, wd_ref[...], preferred_element_type=F32)


def _ffn_kernel(x_ref, g_ref, wg_ref, wu_ref, wd_ref, o_ref):
    o_ref[...] = _swiglu_residual(x_ref[...], g_ref[...], wg_ref, wu_ref, wd_ref)


def _proj_ffn_norm_kernel(x_ref, y_ref, wo_ref, bo_ref, g_ref, wg_ref, wu_ref, wd_ref, gf_ref, o_ref):
    x = x_ref[...] + jnp.dot(y_ref[...], wo_ref[...], preferred_element_type=F32) + bo_ref[...]
    o_ref[...] = _rms(_swiglu_residual(x, g_ref[...], wg_ref, wu_ref, wd_ref), gf_ref[...])


def _ffn_weight_specs():
    return [_const_spec((1, D_MODEL)), _const_spec((D_MODEL, FFN_HIDDEN)), _const_spec((D_MODEL, FFN_HIDDEN)),
            _const_spec((FFN_HIDDEN, D_MODEL))]


def _ffn(x, g, wg, wu, wd, tm=512):
    m = x.shape[0]
    blk = pl.BlockSpec((tm, D_MODEL), lambda i: (i, 0))
    return pl.pallas_call(
        _ffn_kernel, out_shape=jax.ShapeDtypeStruct((m, D_MODEL), F32), grid=(m // tm,),
        in_specs=[blk] + _ffn_weight_specs(), out_specs=blk,
        compiler_params=_cparams(("parallel",)), name="ffn",
    )(x, g, wg, wu, wd)


def _proj_ffn_norm(x, y, wo, bo, g, wg, wu, wd, gf, tm=512):
    m = x.shape[0]
    blk = pl.BlockSpec((tm, D_MODEL), lambda i: (i, 0))
    return pl.pallas_call(
        _proj_ffn_norm_kernel, out_shape=jax.ShapeDtypeStruct((m, D_MODEL), F32), grid=(m // tm,),
        in_specs=[blk, blk, _const_spec((D_MODEL, D_MODEL)), _const_spec((1, D_MODEL))] + _ffn_weight_specs()
        + [_const_spec((1, D_MODEL))],
        out_specs=blk,
        compiler_params=_cparams(("parallel",)), name="proj_ffn_norm",
    )(x, y, wo, bo, g, wg, wu, wd, gf)


def _qkv_kernel(x_ref, g_ref, w_ref, b_ref, q_ref, k_ref, v_ref):
    h = _rms(x_ref[...], g_ref[...]).astype(BF16)
    z = jnp.dot(h, w_ref[...], preferred_element_type=F32) + b_ref[...]
    q_ref[...] = (z[:, :D_MODEL] * (NA_HEAD_DIM ** -0.5 * LOG2_E)).astype(BF16)
    k_ref[...] = z[:, D_MODEL:2 * D_MODEL].astype(BF16)
    v_ref[...] = z[:, 2 * D_MODEL:].astype(BF16)


def _qkv(x, g, w, b, tm=512):
    m = x.shape[0]
    act = jax.ShapeDtypeStruct((m, D_MODEL), BF16)
    blk = pl.BlockSpec((tm, D_MODEL), lambda i: (i, 0))
    return pl.pallas_call(
        _qkv_kernel, out_shape=(act, act, act), grid=(m // tm,),
        in_specs=[blk, _const_spec((1, D_MODEL)), _const_spec((D_MODEL, 3 * D_MODEL)),
                  _const_spec((1, 3 * D_MODEL))],
        out_specs=(blk, blk, blk),
        compiler_params=_cparams(("parallel",)), name="na_qkv",
    )(x, g, w, b)


def _na_kernel(q_ref, k0_ref, k1_ref, k2_ref, v0_ref, v1_ref, v2_ref, bias_ref, o_ref):
    tok = NA_QR * GRID_W
    lane = lax.broadcasted_iota(jnp.int32, (1, 128), 1)
    first = lane < NA_HEAD_DIM
    ones = jnp.ones((NA_KR * GRID_W, 128), BF16)
    for p in range(NA_HEADS // 2):
        sl = slice(128 * p, 128 * (p + 1))
        qp = q_ref[:, sl]
        zero = jnp.zeros_like(qp)
        q2 = jnp.concatenate([jnp.where(first, qp, zero), jnp.where(first, zero, qp)], axis=0)
        kp = jnp.concatenate([k0_ref[:, sl], k1_ref[:, sl], k2_ref[:, sl]], axis=0)
        vp = jnp.concatenate([v0_ref[:, sl], v1_ref[:, sl], v2_ref[:, sl]], axis=0)
        s = lax.dot_general(q2, kp, (((1,), (1,)), ((), ())), preferred_element_type=F32)
        s = s + bias_ref[p]
        e = jnp.exp2(s - jnp.max(s, axis=-1, keepdims=True)).astype(BF16)
        o = jnp.dot(e, jnp.concatenate([vp, ones], axis=1), preferred_element_type=F32)
        o = o[:, :128] * (1.0 / o[:, 128:])
        o_ref[:, sl] = jnp.where(first, o[:tok], o[tok:]).astype(BF16)


def _na_attention(q, k, v, bias):
    tok = NA_QR * GRID_W
    bias = bias.reshape(3, NA_HEADS // 2, 2 * tok, NA_KR * GRID_W)

    def chunk(t):
        return pl.BlockSpec((None, None, tok, D_MODEL),
                            lambda bb, j: (bb, jnp.clip(j - 1, 0, NA_NBLK - 3) + t, 0, 0))

    def btype(bb, j):
        return (jnp.where(j == 0, 0, jnp.where(j == NA_NBLK - 1, 2, 1)), 0, 0, 0)

    qspec = pl.BlockSpec((None, None, tok, D_MODEL), lambda bb, j: (bb, j, 0, 0))
    return pl.pallas_call(
        _na_kernel, out_shape=jax.ShapeDtypeStruct(q.shape, BF16), grid=(BATCH, NA_NBLK),
        in_specs=[qspec, chunk(0), chunk(1), chunk(2), chunk(0), chunk(1), chunk(2),
                  pl.BlockSpec((None, NA_HEADS // 2, 2 * tok, NA_KR * GRID_W), btype,
                               pipeline_mode=pl.Buffered(1))],
        out_specs=qspec,
        compiler_params=_cparams(("parallel", "arbitrary")), name="na_attention",
    )(q, k, k, k, v, v, v, bias)


def kernel(x, norm_mix, norm_ffn, norm_final, hy_w_in, hy_b_in, hy_conv_w, hy_conv_b, hy_f_w1, hy_f_b1, hy_f_w2, hy_f_b2, hy_f_w3, hy_f_b3, hy_f_freq, hy_f_wout, hy_decay, hy_skip, hy_w_out, hy_b_out, na_w_qkv, na_b_qkv, na_rpb, na_w_o, na_b_o, ffn_w_gate, ffn_w_up, ffn_w_down):
    tabs = _fft_tables()
    row = lambda a: a.reshape(1, -1)
    m_tok = BATCH * SEQ

    s, x0 = _hyena_in(x, row(norm_mix[0]), hy_w_in[0].astype(BF16), row(hy_b_in[0]),
                      hy_conv_w[0], row(hy_conv_b[0]))
    col = lambda a: a.reshape(-1, 1)
    feat_t, t_col = _filter_features()
    w1t = jnp.zeros((HY_WIDTH, HY_EMB_PAD), F32).at[:, :HY_EMB_DIM].set(hy_f_w1[0].T)
    g_taps, g_sum = _hyena_filter(feat_t, t_col, w1t, col(hy_f_b1[0]), hy_f_w2[0].T, col(hy_f_b2[0]),
                                  hy_f_w3[0].T, col(hy_f_b3[0]), col(hy_f_freq[0]), hy_f_wout[0], hy_decay[0])
    seq4 = lambda a: a.reshape(BATCH, FFT_H1, FFT_N2, D_MODEL)
    bf = lambda name: jnp.asarray(tabs[name]).astype(BF16)
    ga = _fft_stage1(None, g_taps, tabs["cfr"], tabs["cfi"], tabs["twc"], tabs["tws"])
    sa = _fft_stage1(seq4(s), None, tabs["cr"], tabs["ci"], tabs["twc"], tabs["tws"])
    sb = _fft_stage2(sa, ga, bf("m2"), tabs["fr"], tabs["fi"], tabs["twc_t"], tabs["tws_t"])
    yv = _fft_stage3(sb, seq4(s), seq4(x0), row(hy_skip[0]), g_sum, bf("m3"))
    xf = _out_proj_t(seq4(x), yv, hy_w_out[0].astype(BF16), row(hy_b_out[0])).reshape(m_tok, D_MODEL)
    xf = _ffn(xf, row(norm_ffn[0]), ffn_w_gate[0].astype(BF16), ffn_w_up[0].astype(BF16),
              ffn_w_down[0].astype(BF16))

    q, k, v = _qkv(xf, row(norm_mix[1]), na_w_qkv[0].astype(BF16), row(na_b_qkv[0]))
    bias = _na_bias(na_rpb[0])
    blk = lambda a: a.reshape(BATCH, NA_NBLK, NA_QR * GRID_W, D_MODEL)
    att = _na_attention(blk(q), blk(k), blk(v), bias)
    xf = _proj_ffn_norm(xf, att.reshape(m_tok, D_MODEL), na_w_o[0].astype(BF16), row(na_b_o[0]),
                        row(norm_ffn[1]), ffn_w_gate[1].astype(BF16), ffn_w_up[1].astype(BF16),
                        ffn_w_down[1].astype(BF16), row(norm_final))
    return xf.reshape(BATCH, SEQ, D_MODEL)
```

```python
import functools
import math

import numpy as np
import jax
import jax.numpy as jnp
from jax import lax
from jax.experimental import pallas as pl
from jax.experimental.pallas import tpu as pltpu

F32 = jnp.float32
BF16 = jnp.bfloat16

D_MODEL = 1024
SEQ = 16384
BATCH = 2
FFN_HIDDEN = 2816
GRID_W = 64
RMS_EPS = 1e-6
HY_EMB_DIM = 33
HY_EMB_PAD = 40
HY_N_BANDS = 16
HY_WIDTH = 64
NA_HEADS = 16
NA_HEAD_DIM = 64
NA_KH = 8
NA_KW = 16

FFT_N = 2 * SEQ
FFT_N1 = 256
FFT_N2 = 128
FFT_H1 = FFT_N1 // 2
FFT_NB = 8
FFT_LANES = 512
N_SLABS = D_MODEL // 128
STEP_SLABS = FFT_LANES // 128

NA_ROWS = SEQ // GRID_W
NA_QR = 4
NA_KR = 12
NA_NBLK = NA_ROWS // NA_QR
NEG_BIG = -1e30
LOG2_E = math.log2(math.e)

VMEM_LIMIT = 56 * 1024 * 1024


def _cparams(sem):
    return pltpu.CompilerParams(dimension_semantics=sem, vmem_limit_bytes=VMEM_LIMIT)


def _const_spec(shape):
    nd = len(shape)
    return pl.BlockSpec(shape, lambda *_: (0,) * nd, pipeline_mode=pl.Buffered(1))


def _rms(x, g):
    ms = jnp.mean(x * x, axis=-1, keepdims=True)
    return x * lax.rsqrt(ms + RMS_EPS) * g


@functools.lru_cache(maxsize=None)
def _fft_tables():
    n1, n2, n = FFT_N1, FFT_N2, FFT_N
    k1 = np.arange(n1)[:, None]
    nh = np.arange(FFT_H1)[None, :]
    nf = np.arange(n1)[None, :]
    c = np.exp(-2j * np.pi * (nh * k1) / n1)
    cf = np.exp(-2j * np.pi * (nf * k1) / n1)
    e = np.exp(2j * np.pi * (nh * k1) / n1).T
    m3 = np.block([[e.real, -e.imag], [e.imag, e.real]])
    m3 = m3.reshape(2, FFT_H1, 2 * n1)
    a = np.arange(n2)
    f = np.exp(-2j * np.pi * np.outer(a, a) / n2)
    m2 = np.block([[f.real, -f.imag], [f.imag, f.real]])
    tw = np.exp(-2j * np.pi * np.outer(np.arange(n1), np.arange(n2)) / n)
    as_f = lambda m: np.asarray(m, dtype=np.float32)
    return dict(cr=as_f(c.real), ci=as_f(c.imag), cfr=as_f(cf.real), cfi=as_f(cf.imag), m3=as_f(m3),
                m2=as_f(m2), fr=as_f(f.real), fi=as_f(f.imag),
                twc=as_f(tw.real), tws=as_f(tw.imag),
                twc_t=as_f(tw.real.T), tws_t=as_f(tw.imag.T))


@functools.lru_cache(maxsize=None)
def _filter_features():
    ell = SEQ
    t = np.linspace(0.0, 1.0, ell, dtype=np.float32).astype(np.float64)
    w = (2.0 * math.pi * np.arange(ell, dtype=np.float32) / ell).astype(np.float32).astype(np.float64)
    bands = np.linspace(1e-4, HY_N_BANDS - 1, HY_N_BANDS, dtype=np.float32).astype(np.float64)
    ang = (bands[None, :] * w[:, None]).astype(np.float32).astype(np.float64)
    feat = np.concatenate([t[:, None], np.cos(ang), -np.sin(ang)], axis=-1)
    tab = np.zeros((2 * ell, HY_EMB_PAD), np.float64)
    tab[:ell, :HY_EMB_DIM] = feat
    tab[:ell, HY_EMB_DIM] = 1.0
    tab[ell + 1:, :HY_EMB_DIM] = feat[:0:-1]
    tab[ell + 1:, HY_EMB_DIM] = 1.0
    j, d, i, h = np.meshgrid(np.arange(FFT_N2 // FFT_NB), np.arange(2), np.arange(FFT_NB), np.arange(FFT_H1),
                             indexing="ij")
    perm = (FFT_N2 * (d * FFT_H1 + h) + FFT_NB * j + i).reshape(-1)
    tab = tab[perm]
    return np.asarray(tab.T, dtype=np.float32), np.asarray(tab[:, 0:1], dtype=np.float32)


def _na_bias_plan():
    plan = []
    for j in (0, 1, NA_NBLK - 1):
        base = NA_QR * min(max(j - 1, 0), NA_NBLK - 3)
        per_type = []
        for ri in range(NA_QR):
            r = NA_QR * j + ri
            rs = min(max(r - NA_KH // 2, 0), NA_ROWS - NA_KH)
            row = []
            for m in range(NA_KR // 2):
                kr = base + 2 * m
                row.append((kr - r + (NA_KH - 1), 0 <= kr - rs < NA_KH, 0 <= kr + 1 - rs < NA_KH))
            per_type.append(row)
        plan.append(per_type)
    return plan


def _na_bias_kernel(v_ref, o_ref):
    n_e = 2 * NA_KH
    c = lax.broadcasted_iota(jnp.int32, (GRID_W, 128), 0)
    lane = lax.broadcasted_iota(jnp.int32, (GRID_W, 128), 1)
    kc = lane % GRID_W
    cs = jnp.clip(c - NA_KW // 2, 0, GRID_W - NA_KW)
    col_ok = (kc >= cs) & (kc < cs + NA_KW)
    left = lane < GRID_W
    tiles = []
    for e in range(n_e):
        x = jnp.broadcast_to(v_ref[e:e + 1, :] * LOG2_E, (GRID_W, 128))
        tiles.append(pltpu.roll(x, 128 - (NA_KW - 1), 1, stride=1, stride_axis=0))
    neg = jnp.full((GRID_W, 128), NEG_BIG, F32)
    for t, per_type in enumerate(_na_bias_plan()):
        for ri, row in enumerate(per_type):
            for m, (ro_left, ok_l, ok_r) in enumerate(row):
                if not (ok_l or ok_r):
                    tile = neg
                else:
                    mask = col_ok if (ok_l and ok_r) else (col_ok & left if ok_l else col_ok & ~left)
                    tile = jnp.where(mask, tiles[ro_left + 1], NEG_BIG)
                o_ref[t, ri * GRID_W:(ri + 1) * GRID_W, m * 128:(m + 1) * 128] = tile


def _na_bias(rpb):
    n_e = 2 * NA_KH
    rp = jnp.pad(rpb, ((0, 0), (1, 1), (0, GRID_W - (2 * NA_KW - 1))))
    v = jnp.concatenate([rp[:, :n_e], rp[:, 1:n_e + 1]], axis=-1)
    tok, keys = NA_QR * GRID_W, NA_KR * GRID_W
    return pl.pallas_call(
        _na_bias_kernel, out_shape=jax.ShapeDtypeStruct((3, NA_HEADS, tok, keys), F32), grid=(NA_HEADS,),
        in_specs=[pl.BlockSpec((None, n_e, 128), lambda h: (h, 0, 0))],
        out_specs=pl.BlockSpec((3, None, tok, keys), lambda h: (0, h, 0, 0)),
        compiler_params=_cparams(("parallel",)), name="na_bias",
    )(v)


def _hy_in_kernel(xp_ref, x_ref, xn_ref, g_ref, w_ref, b_ref, cw_ref, cb_ref,
                  s_ref, x0_ref, z_ref, *, tl, nt):
    i = pl.program_id(1)
    xh = jnp.concatenate([xp_ref[...], x_ref[...], xn_ref[...]], axis=0)
    h = _rms(xh, g_ref[...]).astype(BF16)
    z = jnp.dot(h, w_ref[...], preferred_element_type=F32) + b_ref[...]
    row = lax.broadcasted_iota(jnp.int32, (tl + 16, 1), 0)
    inside = ((row >= 8) | (i > 0)) & ((row < tl + 8) | (i < nt - 1))
    z = jnp.where(inside, z, 0.0)
    z_ref[0] = pltpu.roll(z, 1, 0)
    z_ref[1] = z
    z_ref[2] = pltpu.roll(z, tl + 15, 0)

    def conv(c0):
        sl = slice(c0, c0 + D_MODEL)
        rows = slice(8, 8 + tl)
        return (cw_ref[0:1, sl] * z_ref[0, rows, sl] + cw_ref[1:2, sl] * z_ref[1, rows, sl]
                + cw_ref[2:3, sl] * z_ref[2, rows, sl] + cb_ref[:, sl])

    x0 = conv(0)
    s = conv(2 * D_MODEL) * conv(D_MODEL)
    _store_grouped(x0_ref, x0)
    _store_grouped(s_ref, s)


def _store_grouped(ref, val):
    n1s = val.shape[0] // FFT_N2
    for c in range(N_SLABS):
        for jg in range(FFT_N2 // FFT_NB):
            tiles = [val[a * FFT_N2 + jg * FFT_NB:a * FFT_N2 + (jg + 1) * FFT_NB, c * 128:(c + 1) * 128]
                     for a in range(n1s)]
            ref[c, jg] = jnp.concatenate(tiles, axis=0)


def _load_grouped(ref):
    n1s = ref.shape[2] // FFT_NB
    slabs = []
    for c in range(N_SLABS):
        tiles = [ref[c, jg, a * FFT_NB:(a + 1) * FFT_NB, :] for a in range(n1s)
                 for jg in range(FFT_N2 // FFT_NB)]
        slabs.append(jnp.concatenate(tiles, axis=0))
    return jnp.concatenate(slabs, axis=1)


def _grouped_spec(tl):
    rows = tl // FFT_N2 * FFT_NB
    return pl.BlockSpec((None, N_SLABS, FFT_N2 // FFT_NB, rows, 128), lambda bb, i: (bb, 0, 0, i, 0))


def _hyena_in(x, g, w, b, cw, cb, tl=512):
    nt = SEQ // tl
    t8 = tl // 8
    kern = functools.partial(_hy_in_kernel, tl=tl, nt=nt)
    act = jax.ShapeDtypeStruct((BATCH, N_SLABS, FFT_N2 // FFT_NB, FFT_H1 * FFT_NB, 128), F32)
    return pl.pallas_call(
        kern, out_shape=(act, act), grid=(BATCH, nt),
        in_specs=[
            pl.BlockSpec((None, 8, D_MODEL), lambda bb, i: (bb, jnp.maximum(i * t8 - 1, 0), 0)),
            pl.BlockSpec((None, tl, D_MODEL), lambda bb, i: (bb, i, 0)),
            pl.BlockSpec((None, 8, D_MODEL), lambda bb, i: (bb, jnp.minimum((i + 1) * t8, SEQ // 8 - 1), 0)),
            _const_spec((1, D_MODEL)), _const_spec((D_MODEL, 3 * D_MODEL)), _const_spec((1, 3 * D_MODEL)),
            _const_spec((3, 3 * D_MODEL)), _const_spec((1, 3 * D_MODEL)),
        ],
        out_specs=(_grouped_spec(tl), _grouped_spec(tl)),
        scratch_shapes=[pltpu.VMEM((3, tl + 16, 3 * D_MODEL), F32)],
        compiler_params=_cparams(("parallel", "parallel")), name="hyena_in",
    )(x, x, x, g, w, b, cw, cb)


def _dot3(a, b, dims):
    a_hi = a.astype(BF16)
    a_lo = (a - a_hi.astype(F32)).astype(BF16)
    b_hi = b.astype(BF16)
    b_lo = (b - b_hi.astype(F32)).astype(BF16)
    dot = lambda u, v: lax.dot_general(u, v, dims, preferred_element_type=F32)
    return dot(a_hi, b_hi) + dot(a_hi, b_lo) + dot(a_lo, b_hi)


def _filter_kernel(ft_ref, tc_ref, w1_ref, b1_ref, w2_ref, b2_ref, w3_ref, b3_ref, fr_ref, wo_ref, dec_ref,
                   g_ref, sum_ref):
    hp = lax.Precision.HIGHEST
    ft = ft_ref[...]
    fr = fr_ref[...]
    h = jnp.sin(fr * (jnp.dot(w1_ref[...], ft, precision=hp, preferred_element_type=F32) + b1_ref[...]))
    h = jnp.sin(fr * (jnp.dot(w2_ref[...], h, precision=hp, preferred_element_type=F32) + b2_ref[...]))
    h = jnp.sin(fr * (jnp.dot(w3_ref[...], h, precision=hp, preferred_element_type=F32) + b3_ref[...]))
    h = h * ft[HY_EMB_DIM:HY_EMB_DIM + 1, :]
    half = FFT_NB * FFT_H1
    total = jnp.zeros((1, D_MODEL), F32)
    for d in range(2):
        hd = h[:, d * half:(d + 1) * half]
        o = _dot3(hd, wo_ref[:, d * D_MODEL:(d + 1) * D_MODEL], (((0,), (0,)), ((), ())))
        t = tc_ref[d * half:(d + 1) * half, :]
        gq = o * jnp.exp(-t * jnp.abs(dec_ref[d:d + 1, :]))
        total = total + jnp.sum(jnp.abs(gq), axis=0, keepdims=True)
        for i in range(FFT_NB):
            g_ref[i, d * FFT_H1:(d + 1) * FFT_H1, :] = gq[i * FFT_H1:(i + 1) * FFT_H1].astype(BF16)

    @pl.when(pl.program_id(0) == 0)
    def _():
        sum_ref[...] = jnp.zeros_like(sum_ref)

    sum_ref[...] += total


def _hyena_filter(feat_t, t_col, w1t, b1, w2t, b2, w3t, b3, freq, wout, decay):
    tr = 2 * FFT_NB * FFT_H1
    col = lambda: _const_spec((HY_WIDTH, 1))
    return pl.pallas_call(
        _filter_kernel,
        out_shape=(jax.ShapeDtypeStruct((FFT_N2, FFT_N1, D_MODEL), BF16),
                   jax.ShapeDtypeStruct((1, D_MODEL), F32)),
        grid=(FFT_N2 // FFT_NB,),
        in_specs=[
            pl.BlockSpec((HY_EMB_PAD, tr), lambda j: (0, j)),
            pl.BlockSpec((tr, 1), lambda j: (j, 0)),
            _const_spec((HY_WIDTH, HY_EMB_PAD)), col(),
            _const_spec((HY_WIDTH, HY_WIDTH)), col(),
            _const_spec((HY_WIDTH, HY_WIDTH)), col(),
            col(),
            _const_spec((HY_WIDTH, 2 * D_MODEL)),
            _const_spec((2, D_MODEL)),
        ],
        out_specs=(pl.BlockSpec((FFT_NB, FFT_N1, D_MODEL), lambda j: (j, 0, 0)),
                   pl.BlockSpec((1, D_MODEL), lambda j: (0, 0))),
        compiler_params=_cparams(("arbitrary",)), name="hyena_filter",
    )(feat_t, t_col, w1t, b1, w2t, b2, w3t, b3, freq, wout, decay)


def _select_column(tab_ref, idx):
    tab = tab_ref[...]
    lane = lax.broadcasted_iota(jnp.int32, tab.shape, 1)
    return jnp.sum(jnp.where(lane == idx, tab, 0.0), axis=-1, keepdims=True)


def _stack_bf16(re, im):
    return jnp.concatenate([re, im], axis=0).astype(BF16)


def _twiddled(cr, ci, c, s):
    return cr * c - ci * s, cr * s + ci * c


def _gather_rows(ref, lead, i, rows):
    parts = [ref[(*lead, c, pl.ds(i, rows, stride=FFT_NB), slice(None))] for c in range(STEP_SLABS)]
    return jnp.concatenate(parts, axis=1)


def _scatter_rows(ref, i, val):
    rows = val.shape[0]
    for c in range(STEP_SLABS):
        ref[c, pl.ds(i, rows, stride=FFT_NB), :] = val[:, c * 128:(c + 1) * 128]


def _tile_rows(ref, q):
    groups = ref.shape[1]
    parts = [ref[c, :, q * FFT_NB:(q + 1) * FFT_NB, :].reshape(groups * FFT_NB, 128) for c in range(STEP_SLABS)]
    return jnp.concatenate(parts, axis=1)


def _fft1_kernel(s_ref, cr_ref, ci_ref, twc_ref, tws_ref, ar_ref, ai_ref):
    n2_0 = pl.program_id(1) * FFT_NB
    for i in range(FFT_NB):
        pr, pi = _twiddled(cr_ref[...], ci_ref[...],
                           _select_column(twc_ref, n2_0 + i), _select_column(tws_ref, n2_0 + i))
        m = jnp.concatenate([jnp.concatenate([pr, -pi], axis=1), jnp.concatenate([pi, pr], axis=1)], axis=0)
        rhs = _stack_bf16(_gather_rows(s_ref, (0,), i, FFT_H1), _gather_rows(s_ref, (1,), i, FFT_H1))
        a = jnp.dot(m.astype(BF16), rhs, preferred_element_type=F32)
        _scatter_rows(ar_ref, i, a[:FFT_N1])
        _scatter_rows(ai_ref, i, a[FFT_N1:])


def _fft1_filter_kernel(g_ref, cr_ref, ci_ref, twc_ref, tws_ref, ar_ref, ai_ref):
    n2_0 = pl.program_id(1) * FFT_NB
    for i in range(FFT_NB):
        pr, pi = _twiddled(cr_ref[...], ci_ref[...],
                           _select_column(twc_ref, n2_0 + i), _select_column(tws_ref, n2_0 + i))
        m = jnp.concatenate([pr, pi], axis=0).astype(BF16)
        a = jnp.dot(m, g_ref[i], preferred_element_type=F32)
        _scatter_rows(ar_ref, i, a[:FFT_N1])
        _scatter_rows(ai_ref, i, a[FFT_N1:])


def _fft_stage1(s5, g3, cr, ci, twc, tws):
    consts = [_const_spec(cr.shape), _const_spec(ci.shape),
              _const_spec((FFT_N1, FFT_N2)), _const_spec((FFT_N1, FFT_N2))]
    out = jax.ShapeDtypeStruct((N_SLABS, FFT_N2 // FFT_NB, FFT_N1 * FFT_NB, 128), F32)
    oblk = pl.BlockSpec((STEP_SLABS, None, FFT_N1 * FFT_NB, 128), lambda h, j: (h, j, 0, 0))
    common = dict(out_shape=(out, out), grid=(D_MODEL // FFT_LANES, FFT_N2 // FFT_NB), out_specs=(oblk, oblk),
                  compiler_params=_cparams(("parallel", "parallel")))
    if g3 is None:
        return pl.pallas_call(
            _fft1_kernel,
            in_specs=[pl.BlockSpec((2, STEP_SLABS, None, FFT_H1 * FFT_NB, 128),
                                   lambda h, j: (0, h, j, 0, 0))] + consts,
            name="fft_stage1", **common,
        )(s5, cr, ci, twc, tws)
    return pl.pallas_call(
        _fft1_filter_kernel,
        in_specs=[pl.BlockSpec((FFT_NB, FFT_N1, FFT_LANES), lambda h, j: (j, 0, h))] + consts,
        name="fft_stage1_filter", **common,
    )(g3, cr, ci, twc, tws)


def _fft2_kernel(ar_ref, ai_ref, gr_ref, gi_ref, m_ref, fr_ref, fi_ref, twc_ref, tws_ref, br_ref, bi_ref):
    k0 = pl.program_id(1) * FFT_NB
    m = m_ref[...]
    for q in range(FFT_NB):
        z = jnp.dot(m, _stack_bf16(_tile_rows(ar_ref, q), _tile_rows(ai_ref, q)), preferred_element_type=F32)
        g = jnp.dot(m, _stack_bf16(_tile_rows(gr_ref, q), _tile_rows(gi_ref, q)), preferred_element_type=F32)
        zr, zi = z[:FFT_N2], z[FFT_N2:]
        gr, gi = g[:FFT_N2], g[FFT_N2:]
        y = _stack_bf16(zr * gr - zi * gi, zr * gi + zi * gr)
        pr, pi = _twiddled(fr_ref[...], -fi_ref[...],
                           _select_column(twc_ref, k0 + q), -_select_column(tws_ref, k0 + q))
        mc = jnp.concatenate([jnp.concatenate([pr, -pi], axis=1), jnp.concatenate([pi, pr], axis=1)], axis=0)
        bb = jnp.dot(mc.astype(BF16), y, preferred_element_type=F32)
        _scatter_rows(br_ref, q, bb[:FFT_N2])
        _scatter_rows(bi_ref, q, bb[FFT_N2:])


def _fft_stage2(a, ag, m2, fr, fi, twc_t, tws_t):
    blk = pl.BlockSpec((STEP_SLABS, FFT_N2 // FFT_NB, FFT_NB * FFT_NB, 128), lambda h, k: (h, 0, k, 0))
    out = jax.ShapeDtypeStruct((N_SLABS, FFT_N1 // FFT_NB, FFT_N2 * FFT_NB, 128), F32)
    oblk = pl.BlockSpec((STEP_SLABS, None, FFT_N2 * FFT_NB, 128), lambda h, k: (h, k, 0, 0))
    return pl.pallas_call(
        _fft2_kernel, out_shape=(out, out), grid=(D_MODEL // FFT_LANES, FFT_N1 // FFT_NB),
        in_specs=[blk, blk, blk, blk, _const_spec((2 * FFT_N2, 2 * FFT_N2)),
                  _const_spec((FFT_N2, FFT_N2)), _const_spec((FFT_N2, FFT_N2)),
                  _const_spec((FFT_N2, FFT_N1)), _const_spec((FFT_N2, FFT_N1))],
        out_specs=(oblk, oblk),
        compiler_params=_cparams(("parallel", "parallel")), name="fft_stage2",
    )(*a, *ag, m2, fr, fi, twc_t, tws_t)


def _fft3_kernel(br_ref, bi_ref, s_ref, x0_ref, skip_ref, sum_ref, m_ref, y_ref):
    skip = skip_ref[...]
    scale = 1.0 / (sum_ref[...] * FFT_N)
    for i in range(FFT_NB):
        rhs = _stack_bf16(_tile_rows(br_ref, i), _tile_rows(bi_ref, i))
        y = jnp.dot(m_ref[...], rhs, preferred_element_type=F32)
        s = _gather_rows(s_ref, (), i, FFT_H1)
        x0 = _gather_rows(x0_ref, (), i, FFT_H1)
        _scatter_rows(y_ref, i, (y * scale + s * skip) * x0)


def _fft_stage3(b, s5, x05, skip, gsum, m3):
    act = pl.BlockSpec((None, STEP_SLABS, None, FFT_H1 * FFT_NB, 128), lambda h, j, bb: (bb, h, j, 0, 0))
    bblk = pl.BlockSpec((STEP_SLABS, FFT_N1 // FFT_NB, FFT_NB * FFT_NB, 128), lambda h, j, bb: (h, 0, j, 0))
    vec = pl.BlockSpec((1, FFT_LANES), lambda h, j, bb: (0, h))
    return pl.pallas_call(
        _fft3_kernel, out_shape=jax.ShapeDtypeStruct(s5.shape, F32),
        grid=(D_MODEL // FFT_LANES, FFT_N2 // FFT_NB, BATCH),
        in_specs=[bblk, bblk, act, act, vec, vec,
                  pl.BlockSpec((None, FFT_H1, 2 * FFT_N1), lambda h, j, bb: (bb, 0, 0))],
        out_specs=act,
        compiler_params=_cparams(("parallel", "parallel", "arbitrary")), name="fft_stage3",
    )(*b, s5, x05, skip, gsum, m3)


def _swiglu_residual(x, g, wg_ref, wu_ref, wd_ref):
    h = _rms(x, g).astype(BF16)
    gate = jnp.dot(h, wg_ref[...], preferred_element_type=F32)
    up = jnp.dot(h, wu_ref[...], preferred_element_type=F32)
    a = (gate * jax.nn.sigmoid(gate) * up).astype(BF16)
    return x + jnp.dot(a, wd_ref[...], preferred_element_type=F32)


def _proj_ffn_kernel(x_ref, y_ref, wo_ref, bo_ref, g_ref, wg_ref, wu_ref, wd_ref, o_ref):
    y = _load_grouped(y_ref).astype(BF16)
    x = x_ref[...] + jnp.dot(y, wo_ref[...], preferred_element_type=F32) + bo_ref[...]
    o_ref[...] = _swiglu_residual(x, g_ref[...], wg_ref, wu_ref, wd_ref)


def _proj_ffn_norm_kernel(x_ref, y_ref, wo_ref, bo_ref, g_ref, wg_ref, wu_ref, wd_ref, gf_ref, o_ref):
    x = x_ref[...] + jnp.dot(y_ref[...], wo_ref[...], preferred_element_type=F32) + bo_ref[...]
    o_ref[...] = _rms(_swiglu_residual(x, g_ref[...], wg_ref, wu_ref, wd_ref), gf_ref[...])


def _proj_ffn_weight_specs():
    return [_const_spec((D_MODEL, D_MODEL)), _const_spec((1, D_MODEL)), _const_spec((1, D_MODEL)),
            _const_spec((D_MODEL, FFN_HIDDEN)), _const_spec((D_MODEL, FFN_HIDDEN)),
            _const_spec((FFN_HIDDEN, D_MODEL))]


def _proj_ffn(x3, y4, wo, bo, g, wg, wu, wd, tm=512):
    blk = pl.BlockSpec((None, tm, D_MODEL), lambda bb, i: (bb, i, 0))
    return pl.pallas_call(
        _proj_ffn_kernel, out_shape=jax.ShapeDtypeStruct(x3.shape, F32), grid=(BATCH, SEQ // tm),
        in_specs=[blk, _grouped_spec(tm)] + _proj_ffn_weight_specs(),
        out_specs=blk,
        compiler_params=_cparams(("parallel", "parallel")), name="proj_ffn",
    )(x3, y4, wo, bo, g, wg, wu, wd)


def _proj_ffn_norm(x, y, wo, bo, g, wg, wu, wd, gf, tm=512):
    m = x.shape[0]
    blk = pl.BlockSpec((tm, D_MODEL), lambda i: (i, 0))
    return pl.pallas_call(
        _proj_ffn_norm_kernel, out_shape=jax.ShapeDtypeStruct((m, D_MODEL), F32), grid=(m // tm,),
        in_specs=[blk, blk] + _proj_ffn_weight_specs() + [_const_spec((1, D_MODEL))],
        out_specs=blk,
        compiler_params=_cparams(("parallel",)), name="proj_ffn_norm",
    )(x, y, wo, bo, g, wg, wu, wd, gf)


def _qkv_kernel(x_ref, g_ref, w_ref, b_ref, q_ref, k_ref, v_ref):
    h = _rms(x_ref[...], g_ref[...]).astype(BF16)
    z = jnp.dot(h, w_ref[...], preferred_element_type=F32) + b_ref[...]
    q_ref[...] = (z[:, :D_MODEL] * (NA_HEAD_DIM ** -0.5 * LOG2_E)).astype(BF16)
    k_ref[...] = z[:, D_MODEL:2 * D_MODEL].astype(BF16)
    v_ref[...] = z[:, 2 * D_MODEL:].astype(BF16)


def _qkv(x, g, w, b, tm=512):
    m = x.shape[0]
    act = jax.ShapeDtypeStruct((m, D_MODEL), BF16)
    blk = pl.BlockSpec((tm, D_MODEL), lambda i: (i, 0))
    return pl.pallas_call(
        _qkv_kernel, out_shape=(act, act, act), grid=(m // tm,),
        in_specs=[blk, _const_spec((1, D_MODEL)), _const_spec((D_MODEL, 3 * D_MODEL)),
                  _const_spec((1, 3 * D_MODEL))],
        out_specs=(blk, blk, blk),
        compiler_params=_cparams(("parallel",)), name="na_qkv",
    )(x, g, w, b)


def _na_kernel(q_ref, k0_ref, k1_ref, k2_ref, v0_ref, v1_ref, v2_ref, bias_ref, o_ref):
    tok = NA_QR * GRID_W
    lane = lax.broadcasted_iota(jnp.int32, (1, 128), 1)
    first = lane < NA_HEAD_DIM
    ones = jnp.ones((NA_KR * GRID_W, 128), BF16)
    for p in range(NA_HEADS // 2):
        sl = slice(128 * p, 128 * (p + 1))
        qp = q_ref[:, sl]
        zero = jnp.zeros_like(qp)
        q2 = jnp.concatenate([jnp.where(first, qp, zero), jnp.where(first, zero, qp)], axis=0)
        kp = jnp.concatenate([k0_ref[:, sl], k1_ref[:, sl], k2_ref[:, sl]], axis=0)
        vp = jnp.concatenate([v0_ref[:, sl], v1_ref[:, sl], v2_ref[:, sl]], axis=0)
        s = lax.dot_general(q2, kp, (((1,), (1,)), ((), ())), preferred_element_type=F32)
        s = s + bias_ref[p]
        e = jnp.exp2(s - jnp.max(s, axis=-1, keepdims=True)).astype(BF16)
        o = jnp.dot(e, jnp.concatenate([vp, ones], axis=1), preferred_element_type=F32)
        o = o[:, :128] * (1.0 / o[:, 128:])
        o_ref[:, sl] = jnp.where(first, o[:tok], o[tok:]).astype(BF16)


def _na_attention(q, k, v, bias):
    tok = NA_QR * GRID_W
    bias = bias.reshape(3, NA_HEADS // 2, 2 * tok, NA_KR * GRID_W)

    def chunk(t):
        return pl.BlockSpec((None, None, tok, D_MODEL),
                            lambda bb, j: (bb, jnp.clip(j - 1, 0, NA_NBLK - 3) + t, 0, 0))

    def btype(bb, j):
        return (jnp.where(j == 0, 0, jnp.where(j == NA_NBLK - 1, 2, 1)), 0, 0, 0)

    qspec = pl.BlockSpec((None, None, tok, D_MODEL), lambda bb, j: (bb, j, 0, 0))
    return pl.pallas_call(
        _na_kernel, out_shape=jax.ShapeDtypeStruct(q.shape, BF16), grid=(BATCH, NA_NBLK),
        in_specs=[qspec, chunk(0), chunk(1), chunk(2), chunk(0), chunk(1), chunk(2),
                  pl.BlockSpec((None, NA_HEADS // 2, 2 * tok, NA_KR * GRID_W), btype,
                               pipeline_mode=pl.Buffered(1))],
        out_specs=qspec,
        compiler_params=_cparams(("parallel", "arbitrary")), name="na_attention",
    )(q, k, k, k, v, v, v, bias)


def kernel(x, norm_mix, norm_ffn, norm_final, hy_w_in, hy_b_in, hy_conv_w, hy_conv_b, hy_f_w1, hy_f_b1, hy_f_w2, hy_f_b2, hy_f_w3, hy_f_b3, hy_f_freq, hy_f_wout, hy_decay, hy_skip, hy_w_out, hy_b_out, na_w_qkv, na_b_qkv, na_rpb, na_w_o, na_b_o, ffn_w_gate, ffn_w_up, ffn_w_down):
    tabs = _fft_tables()
    row = lambda a: a.reshape(1, -1)
    m_tok = BATCH * SEQ

    s, x0 = _hyena_in(x, row(norm_mix[0]), hy_w_in[0].astype(BF16), row(hy_b_in[0]),
                      hy_conv_w[0], row(hy_conv_b[0]))
    col = lambda a: a.reshape(-1, 1)
    feat_t, t_col = _filter_features()
    w1t = jnp.zeros((HY_WIDTH, HY_EMB_PAD), F32).at[:, :HY_EMB_DIM].set(hy_f_w1[0].T)
    g_taps, g_sum = _hyena_filter(feat_t, t_col, w1t, col(hy_f_b1[0]), hy_f_w2[0].T, col(hy_f_b2[0]),
                                  hy_f_w3[0].T, col(hy_f_b3[0]), col(hy_f_freq[0]), hy_f_wout[0], hy_decay[0])
    bf = lambda name: jnp.asarray(tabs[name]).astype(BF16)
    ga = _fft_stage1(None, g_taps, tabs["cfr"], tabs["cfi"], tabs["twc"], tabs["tws"])
    sa = _fft_stage1(s, None, tabs["cr"], tabs["ci"], tabs["twc"], tabs["tws"])
    sb = _fft_stage2(sa, ga, bf("m2"), tabs["fr"], tabs["fi"], tabs["twc_t"], tabs["tws_t"])
    yv = _fft_stage3(sb, s, x0, row(hy_skip[0]), g_sum, bf("m3"))
    xf = _proj_ffn(x, yv, hy_w_out[0].astype(BF16), row(hy_b_out[0]),
                   row(norm_ffn[0]), ffn_w_gate[0].astype(BF16), ffn_w_up[0].astype(BF16),
                   ffn_w_down[0].astype(BF16)).reshape(m_tok, D_MODEL)

    q, k, v = _qkv(xf, row(norm_mix[1]), na_w_qkv[0].astype(BF16), row(na_b_qkv[0]))
    bias = _na_bias(na_rpb[0])
    blk = lambda a: a.reshape(BATCH, NA_NBLK, NA_QR * GRID_W, D_MODEL)
    att = _na_attention(blk(q), blk(k), blk(v), bias)
    xf = _proj_ffn_norm(xf, att.reshape(m_tok, D_MODEL), na_w_o[0].astype(BF16), row(na_b_o[0]),
                        row(norm_ffn[1]), ffn_w_gate[1].astype(BF16), ffn_w_up[1].astype(BF16),
                        ffn_w_down[1].astype(BF16), row(norm_final))
    return xf.reshape(BATCH, SEQ, D_MODEL)
```

```python
import functools
import math

import numpy as np
import jax
import jax.numpy as jnp
from jax import lax
from jax.experimental import pallas as pl
from jax.experimental.pallas import tpu as pltpu

F32 = jnp.float32
BF16 = jnp.bfloat16

D_MODEL = 1024
SEQ = 16384
BATCH = 2
FFN_HIDDEN = 2816
GRID_W = 64
RMS_EPS = 1e-6
HY_EMB_DIM = 33
HY_EMB_PAD = 40
HY_N_BANDS = 16
HY_WIDTH = 64
NA_HEADS = 16
NA_HEAD_DIM = 64
NA_KH = 8
NA_KW = 16

FFT_N = 2 * SEQ
FFT_N1 = 256
FFT_N2 = 128
FFT_H1 = FFT_N1 // 2
FFT_NB = 8
FFT_LANES = 512
N_SLABS = D_MODEL // 128
STEP_SLABS = FFT_LANES // 128

NA_ROWS = SEQ // GRID_W
NA_QR = 4
NA_KR = 12
NA_NBLK = NA_ROWS // NA_QR
NEG_BIG = -1e30
LOG2_E = math.log2(math.e)

VMEM_LIMIT = 56 * 1024 * 1024


def _cparams(sem):
    return pltpu.CompilerParams(dimension_semantics=sem, vmem_limit_bytes=VMEM_LIMIT)


def _const_spec(shape):
    nd = len(shape)
    return pl.BlockSpec(shape, lambda *_: (0,) * nd, pipeline_mode=pl.Buffered(1))


def _rms(x, g):
    ms = jnp.mean(x * x, axis=-1, keepdims=True)
    return x * lax.rsqrt(ms + RMS_EPS) * g


@functools.lru_cache(maxsize=None)
def _fft_tables():
    n1, n2, n = FFT_N1, FFT_N2, FFT_N
    k1 = np.arange(n1)[:, None]
    nh = np.arange(FFT_H1)[None, :]
    nf = np.arange(n1)[None, :]
    c = np.exp(-2j * np.pi * (nh * k1) / n1)
    cf = np.exp(-2j * np.pi * (nf * k1) / n1)
    e = np.exp(2j * np.pi * (nh * k1) / n1).T
    m3 = np.block([[e.real, -e.imag], [e.imag, e.real]])
    m3 = m3.reshape(2, FFT_H1, 2 * n1)
    a = np.arange(n2)
    f = np.exp(-2j * np.pi * np.outer(a, a) / n2)
    m2 = np.block([[f.real, -f.imag], [f.imag, f.real]])
    tw = np.exp(-2j * np.pi * np.outer(np.arange(n1), np.arange(n2)) / n)
    as_f = lambda m: np.asarray(m, dtype=np.float32)
    return dict(cr=as_f(c.real), ci=as_f(c.imag), cfr=as_f(cf.real), cfi=as_f(cf.imag), m3=as_f(m3),
                m2=as_f(m2), fr=as_f(f.real), fi=as_f(f.imag),
                twc=as_f(tw.real), tws=as_f(tw.imag),
                twc_t=as_f(tw.real.T), tws_t=as_f(tw.imag.T))


@functools.lru_cache(maxsize=None)
def _filter_features():
    ell = SEQ
    t = np.linspace(0.0, 1.0, ell, dtype=np.float32).astype(np.float64)
    w = (2.0 * math.pi * np.arange(ell, dtype=np.float32) / ell).astype(np.float32).astype(np.float64)
    bands = np.linspace(1e-4, HY_N_BANDS - 1, HY_N_BANDS, dtype=np.float32).astype(np.float64)
    ang = (bands[None, :] * w[:, None]).astype(np.float32).astype(np.float64)
    feat = np.concatenate([t[:, None], np.cos(ang), -np.sin(ang)], axis=-1)
    tab = np.zeros((2 * ell, HY_EMB_PAD), np.float64)
    tab[:ell, :HY_EMB_DIM] = feat
    tab[:ell, HY_EMB_DIM] = 1.0
    tab[ell + 1:, :HY_EMB_DIM] = feat[:0:-1]
    tab[ell + 1:, HY_EMB_DIM] = 1.0
    j, d, i, h = np.meshgrid(np.arange(FFT_N2 // FFT_NB), np.arange(2), np.arange(FFT_NB), np.arange(FFT_H1),
                             indexing="ij")
    perm = (FFT_N2 * (d * FFT_H1 + h) + FFT_NB * j + i).reshape(-1)
    tab = tab[perm]
    return np.asarray(tab.T, dtype=np.float32), np.asarray(tab[:, 0:1], dtype=np.float32)


def _na_bias_plan():
    plan = []
    for j in (0, 1, NA_NBLK - 1):
        base = NA_QR * min(max(j - 1, 0), NA_NBLK - 3)
        per_type = []
        for ri in range(NA_QR):
            r = NA_QR * j + ri
            rs = min(max(r - NA_KH // 2, 0), NA_ROWS - NA_KH)
            row = []
            for m in range(NA_KR // 2):
                kr = base + 2 * m
                row.append((kr - r + (NA_KH - 1), 0 <= kr - rs < NA_KH, 0 <= kr + 1 - rs < NA_KH))
            per_type.append(row)
        plan.append(per_type)
    return plan


def _na_bias_kernel(v_ref, o_ref):
    n_e = 2 * NA_KH
    c = lax.broadcasted_iota(jnp.int32, (GRID_W, 128), 0)
    lane = lax.broadcasted_iota(jnp.int32, (GRID_W, 128), 1)
    kc = lane % GRID_W
    cs = jnp.clip(c - NA_KW // 2, 0, GRID_W - NA_KW)
    col_ok = (kc >= cs) & (kc < cs + NA_KW)
    left = lane < GRID_W
    tiles = []
    for e in range(n_e):
        x = jnp.broadcast_to(v_ref[e:e + 1, :] * LOG2_E, (GRID_W, 128))
        tiles.append(pltpu.roll(x, 128 - (NA_KW - 1), 1, stride=1, stride_axis=0))
    neg = jnp.full((GRID_W, 128), NEG_BIG, F32)
    for t, per_type in enumerate(_na_bias_plan()):
        for ri, row in enumerate(per_type):
            for m, (ro_left, ok_l, ok_r) in enumerate(row):
                if not (ok_l or ok_r):
                    tile = neg
                else:
                    mask = col_ok if (ok_l and ok_r) else (col_ok & left if ok_l else col_ok & ~left)
                    tile = jnp.where(mask, tiles[ro_left + 1], NEG_BIG)
                o_ref[t, ri * GRID_W:(ri + 1) * GRID_W, m * 128:(m + 1) * 128] = tile


def _na_bias(rpb):
    n_e = 2 * NA_KH
    rp = jnp.pad(rpb, ((0, 0), (1, 1), (0, GRID_W - (2 * NA_KW - 1))))
    v = jnp.concatenate([rp[:, :n_e], rp[:, 1:n_e + 1]], axis=-1)
    tok, keys = NA_QR * GRID_W, NA_KR * GRID_W
    return pl.pallas_call(
        _na_bias_kernel, out_shape=jax.ShapeDtypeStruct((3, NA_HEADS, tok, keys), F32), grid=(NA_HEADS,),
        in_specs=[pl.BlockSpec((None, n_e, 128), lambda h: (h, 0, 0))],
        out_specs=pl.BlockSpec((3, None, tok, keys), lambda h: (0, h, 0, 0)),
        compiler_params=_cparams(("parallel",)), name="na_bias",
    )(v)


def _hy_in_kernel(xp_ref, x_ref, xn_ref, g_ref, w_ref, b_ref, cw_ref, cb_ref,
                  s_ref, x0_ref, z_ref, *, tl, nt):
    i = pl.program_id(1)
    xh = jnp.concatenate([xp_ref[...], x_ref[...], xn_ref[...]], axis=0)
    h = _rms(xh, g_ref[...]).astype(BF16)
    b = b_ref[...]
    row = lax.broadcasted_iota(jnp.int32, (tl + 16, 1), 0)
    inside = ((row >= 8) | (i > 0)) & ((row < tl + 8) | (i < nt - 1))
    z = jnp.where(inside, jnp.dot(h, w_ref[...], preferred_element_type=F32), -b)
    z_ref[0] = pltpu.roll(z, 1, 0)
    z_ref[1] = z
    z_ref[2] = pltpu.roll(z, tl + 15, 0)
    bias = cb_ref[...] + b * (cw_ref[0:1, :] + cw_ref[1:2, :] + cw_ref[2:3, :])

    def conv(c0):
        sl = slice(c0, c0 + D_MODEL)
        rows = slice(8, 8 + tl)
        return (cw_ref[0:1, sl] * z_ref[0, rows, sl] + cw_ref[1:2, sl] * z_ref[1, rows, sl]
                + cw_ref[2:3, sl] * z_ref[2, rows, sl] + bias[:, sl])

    x0_ref[...] = conv(0).astype(BF16)
    _store_grouped(s_ref, conv(2 * D_MODEL) * conv(D_MODEL))


def _store_grouped(ref, val):
    n1s = val.shape[0] // FFT_N2
    for c in range(N_SLABS):
        for jg in range(FFT_N2 // FFT_NB):
            tiles = [val[a * FFT_N2 + jg * FFT_NB:a * FFT_N2 + (jg + 1) * FFT_NB, c * 128:(c + 1) * 128]
                     for a in range(n1s)]
            ref[c, jg] = jnp.concatenate(tiles, axis=0)


def _load_grouped(ref):
    n1s = ref.shape[2] // FFT_NB
    slabs = []
    for c in range(N_SLABS):
        tiles = [ref[c, jg, a * FFT_NB:(a + 1) * FFT_NB, :] for a in range(n1s)
                 for jg in range(FFT_N2 // FFT_NB)]
        slabs.append(jnp.concatenate(tiles, axis=0))
    return jnp.concatenate(slabs, axis=1)


def _grouped_spec(tl):
    rows = tl // FFT_N2 * FFT_NB
    return pl.BlockSpec((None, N_SLABS, FFT_N2 // FFT_NB, rows, 128), lambda bb, i: (bb, 0, 0, i, 0))


def _hyena_in(x, g, w, b, cw, cb, tl=512):
    nt = SEQ // tl
    t8 = tl // 8
    kern = functools.partial(_hy_in_kernel, tl=tl, nt=nt)
    s_out = jax.ShapeDtypeStruct((BATCH, N_SLABS, FFT_N2 // FFT_NB, FFT_H1 * FFT_NB, 128), F32)
    x0_out = jax.ShapeDtypeStruct((BATCH, SEQ, D_MODEL), BF16)
    return pl.pallas_call(
        kern, out_shape=(s_out, x0_out), grid=(BATCH, nt),
        in_specs=[
            pl.BlockSpec((None, 8, D_MODEL), lambda bb, i: (bb, jnp.maximum(i * t8 - 1, 0), 0)),
            pl.BlockSpec((None, tl, D_MODEL), lambda bb, i: (bb, i, 0)),
            pl.BlockSpec((None, 8, D_MODEL), lambda bb, i: (bb, jnp.minimum((i + 1) * t8, SEQ // 8 - 1), 0)),
            _const_spec((1, D_MODEL)), _const_spec((D_MODEL, 3 * D_MODEL)), _const_spec((1, 3 * D_MODEL)),
            _const_spec((3, 3 * D_MODEL)), _const_spec((1, 3 * D_MODEL)),
        ],
        out_specs=(_grouped_spec(tl), pl.BlockSpec((None, tl, D_MODEL), lambda bb, i: (bb, i, 0))),
        scratch_shapes=[pltpu.VMEM((3, tl + 16, 3 * D_MODEL), F32)],
        compiler_params=_cparams(("parallel", "parallel")), name="hyena_in",
    )(x, x, x, g, w, b, cw, cb)


def _dot3(a, b, dims):
    a_hi = a.astype(BF16)
    a_lo = (a - a_hi.astype(F32)).astype(BF16)
    b_hi = b.astype(BF16)
    b_lo = (b - b_hi.astype(F32)).astype(BF16)
    dot = lambda u, v: lax.dot_general(u, v, dims, preferred_element_type=F32)
    return dot(a_hi, b_hi) + dot(a_hi, b_lo) + dot(a_lo, b_hi)


def _filter_kernel(ft_ref, tc_ref, w1_ref, b1_ref, w2_ref, b2_ref, w3_ref, b3_ref, fr_ref, wo_ref, dec_ref,
                   g_ref, sum_ref):
    hp = lax.Precision.HIGHEST
    ft = ft_ref[...]
    fr = fr_ref[...]
    h = jnp.sin(fr * (jnp.dot(w1_ref[...], ft, precision=hp, preferred_element_type=F32) + b1_ref[...]))
    h = jnp.sin(fr * (jnp.dot(w2_ref[...], h, precision=hp, preferred_element_type=F32) + b2_ref[...]))
    h = jnp.sin(fr * (jnp.dot(w3_ref[...], h, precision=hp, preferred_element_type=F32) + b3_ref[...]))
    h = h * ft[HY_EMB_DIM:HY_EMB_DIM + 1, :]
    half = FFT_NB * FFT_H1
    total = jnp.zeros((1, D_MODEL), F32)
    for d in range(2):
        hd = h[:, d * half:(d + 1) * half]
        o = _dot3(hd, wo_ref[:, d * D_MODEL:(d + 1) * D_MODEL], (((0,), (0,)), ((), ())))
        t = tc_ref[d * half:(d + 1) * half, :]
        gq = o * jnp.exp(-t * jnp.abs(dec_ref[d:d + 1, :]))
        total = total + jnp.sum(jnp.abs(gq), axis=0, keepdims=True)
        for i in range(FFT_NB):
            g_ref[i, d * FFT_H1:(d + 1) * FFT_H1, :] = gq[i * FFT_H1:(i + 1) * FFT_H1].astype(BF16)

    @pl.when(pl.program_id(0) == 0)
    def _():
        sum_ref[...] = jnp.zeros_like(sum_ref)

    sum_ref[...] += total


def _hyena_filter(feat_t, t_col, w1t, b1, w2t, b2, w3t, b3, freq, wout, decay):
    tr = 2 * FFT_NB * FFT_H1
    col = lambda: _const_spec((HY_WIDTH, 1))
    return pl.pallas_call(
        _filter_kernel,
        out_shape=(jax.ShapeDtypeStruct((FFT_N2, FFT_N1, D_MODEL), BF16),
                   jax.ShapeDtypeStruct((1, D_MODEL), F32)),
        grid=(FFT_N2 // FFT_NB,),
        in_specs=[
            pl.BlockSpec((HY_EMB_PAD, tr), lambda j: (0, j)),
            pl.BlockSpec((tr, 1), lambda j: (j, 0)),
            _const_spec((HY_WIDTH, HY_EMB_PAD)), col(),
            _const_spec((HY_WIDTH, HY_WIDTH)), col(),
            _const_spec((HY_WIDTH, HY_WIDTH)), col(),
            col(),
            _const_spec((HY_WIDTH, 2 * D_MODEL)),
            _const_spec((2, D_MODEL)),
        ],
        out_specs=(pl.BlockSpec((FFT_NB, FFT_N1, D_MODEL), lambda j: (j, 0, 0)),
                   pl.BlockSpec((1, D_MODEL), lambda j: (0, 0))),
        compiler_params=_cparams(("arbitrary",)), name="hyena_filter",
    )(feat_t, t_col, w1t, b1, w2t, b2, w3t, b3, freq, wout, decay)


def _select_column(tab_ref, idx):
    tab = tab_ref[...]
    lane = lax.broadcasted_iota(jnp.int32, tab.shape, 1)
    return jnp.sum(jnp.where(lane == idx, tab, 0.0), axis=-1, keepdims=True)


def _stack_bf16(re, im):
    return jnp.concatenate([re, im], axis=0).astype(BF16)


def _twiddled(cr, ci, c, s):
    return cr * c - ci * s, cr * s + ci * c


def _gather_rows(ref, lead, i, rows):
    slabs = ref.shape[len(lead)]
    parts = [ref[(*lead, c, pl.ds(i, rows, stride=FFT_NB), slice(None))] for c in range(slabs)]
    return jnp.concatenate(parts, axis=1)


def _scatter_rows(ref, i, val):
    rows = val.shape[0]
    for c in range(ref.shape[0]):
        ref[c, pl.ds(i, rows, stride=FFT_NB), :] = val[:, c * 128:(c + 1) * 128]


def _tile_rows(ref, q):
    slabs, groups = ref.shape[:2]
    parts = [ref[c, :, q * FFT_NB:(q + 1) * FFT_NB, :].reshape(groups * FFT_NB, 128) for c in range(slabs)]
    return jnp.concatenate(parts, axis=1)


def _fft1_kernel(s_ref, cr_ref, ci_ref, twc_ref, tws_ref, ar_ref, ai_ref):
    n2_0 = pl.program_id(1) * FFT_NB
    for i in range(FFT_NB):
        pr, pi = _twiddled(cr_ref[...], ci_ref[...],
                           _select_column(twc_ref, n2_0 + i), _select_column(tws_ref, n2_0 + i))
        m = jnp.concatenate([jnp.concatenate([pr, -pi], axis=1), jnp.concatenate([pi, pr], axis=1)], axis=0)
        rhs = _stack_bf16(_gather_rows(s_ref, (0,), i, FFT_H1), _gather_rows(s_ref, (1,), i, FFT_H1))
        a = jnp.dot(m.astype(BF16), rhs, preferred_element_type=F32)
        _scatter_rows(ar_ref, i, a[:FFT_N1])
        _scatter_rows(ai_ref, i, a[FFT_N1:])


def _fft1_filter_kernel(g_ref, cr_ref, ci_ref, twc_ref, tws_ref, ar_ref, ai_ref):
    n2_0 = pl.program_id(1) * FFT_NB
    for i in range(FFT_NB):
        pr, pi = _twiddled(cr_ref[...], ci_ref[...],
                           _select_column(twc_ref, n2_0 + i), _select_column(tws_ref, n2_0 + i))
        m = jnp.concatenate([pr, pi], axis=0).astype(BF16)
        a = jnp.dot(m, g_ref[i], preferred_element_type=F32)
        _scatter_rows(ar_ref, i, a[:FFT_N1])
        _scatter_rows(ai_ref, i, a[FFT_N1:])


def _fft_stage1(s5, g3, cr, ci, twc, tws):
    consts = [_const_spec(cr.shape), _const_spec(ci.shape),
              _const_spec((FFT_N1, FFT_N2)), _const_spec((FFT_N1, FFT_N2))]
    out = jax.ShapeDtypeStruct((N_SLABS, FFT_N2 // FFT_NB, FFT_N1 * FFT_NB, 128), F32)
    oblk = pl.BlockSpec((STEP_SLABS, None, FFT_N1 * FFT_NB, 128), lambda h, j: (h, j, 0, 0))
    common = dict(out_shape=(out, out), grid=(D_MODEL // FFT_LANES, FFT_N2 // FFT_NB), out_specs=(oblk, oblk),
                  compiler_params=_cparams(("parallel", "parallel")))
    if g3 is None:
        return pl.pallas_call(
            _fft1_kernel,
            in_specs=[pl.BlockSpec((2, STEP_SLABS, None, FFT_H1 * FFT_NB, 128),
                                   lambda h, j: (0, h, j, 0, 0))] + consts,
            name="fft_stage1", **common,
        )(s5, cr, ci, twc, tws)
    return pl.pallas_call(
        _fft1_filter_kernel,
        in_specs=[pl.BlockSpec((FFT_NB, FFT_N1, FFT_LANES), lambda h, j: (j, 0, h))] + consts,
        name="fft_stage1_filter", **common,
    )(g3, cr, ci, twc, tws)


def _fft2_kernel(ar_ref, ai_ref, gr_ref, gi_ref, m_ref, fr_ref, fi_ref, twc_ref, tws_ref, br_ref, bi_ref):
    k0 = pl.program_id(1) * FFT_NB
    m = m_ref[...]
    for q in range(FFT_NB):
        z = jnp.dot(m, _stack_bf16(_tile_rows(ar_ref, q), _tile_rows(ai_ref, q)), preferred_element_type=F32)
        g = jnp.dot(m, _stack_bf16(_tile_rows(gr_ref, q), _tile_rows(gi_ref, q)), preferred_element_type=F32)
        zr, zi = z[:FFT_N2], z[FFT_N2:]
        gr, gi = g[:FFT_N2], g[FFT_N2:]
        y = _stack_bf16(zr * gr - zi * gi, zr * gi + zi * gr)
        pr, pi = _twiddled(fr_ref[...], -fi_ref[...],
                           _select_column(twc_ref, k0 + q), -_select_column(tws_ref, k0 + q))
        mc = jnp.concatenate([jnp.concatenate([pr, -pi], axis=1), jnp.concatenate([pi, pr], axis=1)], axis=0)
        bb = jnp.dot(mc.astype(BF16), y, preferred_element_type=F32)
        _scatter_rows(br_ref, q, bb[:FFT_N2])
        _scatter_rows(bi_ref, q, bb[FFT_N2:])


def _fft_stage2(a, ag, m2, fr, fi, twc_t, tws_t):
    blk = pl.BlockSpec((STEP_SLABS, FFT_N2 // FFT_NB, FFT_NB * FFT_NB, 128), lambda h, k: (h, 0, k, 0))
    out = jax.ShapeDtypeStruct((N_SLABS, FFT_N1 // FFT_NB, FFT_N2 * FFT_NB, 128), F32)
    oblk = pl.BlockSpec((STEP_SLABS, None, FFT_N2 * FFT_NB, 128), lambda h, k: (h, k, 0, 0))
    return pl.pallas_call(
        _fft2_kernel, out_shape=(out, out), grid=(D_MODEL // FFT_LANES, FFT_N1 // FFT_NB),
        in_specs=[blk, blk, blk, blk, _const_spec((2 * FFT_N2, 2 * FFT_N2)),
                  _const_spec((FFT_N2, FFT_N2)), _const_spec((FFT_N2, FFT_N2)),
                  _const_spec((FFT_N2, FFT_N1)), _const_spec((FFT_N2, FFT_N1))],
        out_specs=(oblk, oblk),
        compiler_params=_cparams(("parallel", "parallel")), name="fft_stage2",
    )(*a, *ag, m2, fr, fi, twc_t, tws_t)


def _fft3_kernel(br_ref, bi_ref, s_ref, skip_ref, sum_ref, m_ref, y_ref):
    skip = skip_ref[...]
    scale = 1.0 / (sum_ref[...] * FFT_N)
    for i in range(FFT_NB):
        rhs = _stack_bf16(_tile_rows(br_ref, i), _tile_rows(bi_ref, i))
        y = jnp.dot(m_ref[...], rhs, preferred_element_type=F32)
        _scatter_rows(y_ref, i, y * scale + _gather_rows(s_ref, (), i, FFT_H1) * skip)


def _fft_stage3(b, s5, skip, gsum, m3):
    act = pl.BlockSpec((None, STEP_SLABS, None, FFT_H1 * FFT_NB, 128), lambda h, j, bb: (bb, h, j, 0, 0))
    bblk = pl.BlockSpec((STEP_SLABS, FFT_N1 // FFT_NB, FFT_NB * FFT_NB, 128), lambda h, j, bb: (h, 0, j, 0))
    vec = pl.BlockSpec((1, FFT_LANES), lambda h, j, bb: (0, h))
    return pl.pallas_call(
        _fft3_kernel, out_shape=jax.ShapeDtypeStruct(s5.shape, F32),
        grid=(D_MODEL // FFT_LANES, FFT_N2 // FFT_NB, BATCH),
        in_specs=[bblk, bblk, act, vec, vec,
                  pl.BlockSpec((None, FFT_H1, 2 * FFT_N1), lambda h, j, bb: (bb, 0, 0))],
        out_specs=act,
        compiler_params=_cparams(("parallel", "parallel", "arbitrary")), name="fft_stage3",
    )(*b, s5, skip, gsum, m3)


def _swiglu_residual(x, g, wg_ref, wu_ref, wd_ref):
    h = _rms(x, g).astype(BF16)
    gate = jnp.dot(h, wg_ref[...], preferred_element_type=F32)
    up = jnp.dot(h, wu_ref[...], preferred_element_type=F32)
    a = (gate * jax.nn.sigmoid(gate) * up).astype(BF16)
    return x + jnp.dot(a, wd_ref[...], preferred_element_type=F32)


def _proj_ffn_kernel(x_ref, y_ref, x0_ref, wo_ref, bo_ref, g_ref, wg_ref, wu_ref, wd_ref, o_ref):
    y = (_load_grouped(y_ref) * x0_ref[...].astype(F32)).astype(BF16)
    x = x_ref[...] + jnp.dot(y, wo_ref[...], preferred_element_type=F32) + bo_ref[...]
    o_ref[...] = _swiglu_residual(x, g_ref[...], wg_ref, wu_ref, wd_ref)


def _proj_ffn_norm_kernel(x_ref, y_ref, wo_ref, bo_ref, g_ref, wg_ref, wu_ref, wd_ref, gf_ref, o_ref):
    x = x_ref[...] + jnp.dot(y_ref[...], wo_ref[...], preferred_element_type=F32) + bo_ref[...]
    o_ref[...] = _rms(_swiglu_residual(x, g_ref[...], wg_ref, wu_ref, wd_ref), gf_ref[...])


def _proj_ffn_weight_specs():
    return [_const_spec((D_MODEL, D_MODEL)), _const_spec((1, D_MODEL)), _const_spec((1, D_MODEL)),
            _const_spec((D_MODEL, FFN_HIDDEN)), _const_spec((D_MODEL, FFN_HIDDEN)),
            _const_spec((FFN_HIDDEN, D_MODEL))]


def _proj_ffn(x3, y5, x0, wo, bo, g, wg, wu, wd, tm=512):
    blk = pl.BlockSpec((None, tm, D_MODEL), lambda bb, i: (bb, i, 0))
    return pl.pallas_call(
        _proj_ffn_kernel, out_shape=jax.ShapeDtypeStruct(x3.shape, F32), grid=(BATCH, SEQ // tm),
        in_specs=[blk, _grouped_spec(tm), blk] + _proj_ffn_weight_specs(),
        out_specs=blk,
        compiler_params=_cparams(("parallel", "parallel")), name="proj_ffn",
    )(x3, y5, x0, wo, bo, g, wg, wu, wd)


def _proj_ffn_norm(x, y, wo, bo, g, wg, wu, wd, gf, tm=512):
    m = x.shape[0]
    blk = pl.BlockSpec((tm, D_MODEL), lambda i: (i, 0))
    return pl.pallas_call(
        _proj_ffn_norm_kernel, out_shape=jax.ShapeDtypeStruct((m, D_MODEL), F32), grid=(m // tm,),
        in_specs=[blk, blk] + _proj_ffn_weight_specs() + [_const_spec((1, D_MODEL))],
        out_specs=blk,
        compiler_params=_cparams(("parallel",)), name="proj_ffn_norm",
    )(x, y, wo, bo, g, wg, wu, wd, gf)


def _qkv_kernel(x_ref, g_ref, w_ref, b_ref, q_ref, k_ref, v_ref):
    h = _rms(x_ref[...], g_ref[...]).astype(BF16)
    z = jnp.dot(h, w_ref[...], preferred_element_type=F32) + b_ref[...]
    q_ref[...] = (z[:, :D_MODEL] * (NA_HEAD_DIM ** -0.5 * LOG2_E)).astype(BF16)
    k_ref[...] = z[:, D_MODEL:2 * D_MODEL].astype(BF16)
    v_ref[...] = z[:, 2 * D_MODEL:].astype(BF16)


def _qkv(x, g, w, b, tm=1024):
    m = x.shape[0]
    act = jax.ShapeDtypeStruct((m, D_MODEL), BF16)
    blk = pl.BlockSpec((tm, D_MODEL), lambda i: (i, 0))
    return pl.pallas_call(
        _qkv_kernel, out_shape=(act, act, act), grid=(m // tm,),
        in_specs=[blk, _const_spec((1, D_MODEL)), _const_spec((D_MODEL, 3 * D_MODEL)),
                  _const_spec((1, 3 * D_MODEL))],
        out_specs=(blk, blk, blk),
        compiler_params=_cparams(("parallel",)), name="na_qkv",
    )(x, g, w, b)


def _na_kernel(q_ref, k0_ref, k1_ref, k2_ref, v0_ref, v1_ref, v2_ref, bias_ref, o_ref):
    tok = NA_QR * GRID_W
    lane = lax.broadcasted_iota(jnp.int32, (1, 128), 1)
    first = lane < NA_HEAD_DIM
    ones = jnp.ones((NA_KR * GRID_W, 128), BF16)
    for p in range(NA_HEADS // 2):
        sl = slice(128 * p, 128 * (p + 1))
        qp = q_ref[:, sl]
        zero = jnp.zeros_like(qp)
        q2 = jnp.concatenate([jnp.where(first, qp, zero), jnp.where(first, zero, qp)], axis=0)
        kp = jnp.concatenate([k0_ref[:, sl], k1_ref[:, sl], k2_ref[:, sl]], axis=0)
        vp = jnp.concatenate([v0_ref[:, sl], v1_ref[:, sl], v2_ref[:, sl]], axis=0)
        s = lax.dot_general(q2, kp, (((1,), (1,)), ((), ())), preferred_element_type=F32)
        s = s + bias_ref[p]
        e = jnp.exp2(s - jnp.max(s, axis=-1, keepdims=True)).astype(BF16)
        o = jnp.dot(e, jnp.concatenate([vp, ones], axis=1), preferred_element_type=F32)
        o = o[:, :128] * (1.0 / o[:, 128:])
        o_ref[:, sl] = jnp.where(first, o[:tok], o[tok:]).astype(BF16)


def _na_attention(q, k, v, bias):
    tok = NA_QR * GRID_W
    bias = bias.reshape(3, NA_HEADS // 2, 2 * tok, NA_KR * GRID_W)

    def chunk(t):
        return pl.BlockSpec((None, None, tok, D_MODEL),
                            lambda bb, j: (bb, jnp.clip(j - 1, 0, NA_NBLK - 3) + t, 0, 0))

    def btype(bb, j):
        return (jnp.where(j == 0, 0, jnp.where(j == NA_NBLK - 1, 2, 1)), 0, 0, 0)

    qspec = pl.BlockSpec((None, None, tok, D_MODEL), lambda bb, j: (bb, j, 0, 0))
    return pl.pallas_call(
        _na_kernel, out_shape=jax.ShapeDtypeStruct(q.shape, BF16), grid=(BATCH, NA_NBLK),
        in_specs=[qspec, chunk(0), chunk(1), chunk(2), chunk(0), chunk(1), chunk(2),
                  pl.BlockSpec((None, NA_HEADS // 2, 2 * tok, NA_KR * GRID_W), btype,
                               pipeline_mode=pl.Buffered(1))],
        out_specs=qspec,
        compiler_params=_cparams(("parallel", "arbitrary")), name="na_attention",
    )(q, k, k, k, v, v, v, bias)


def kernel(x, norm_mix, norm_ffn, norm_final, hy_w_in, hy_b_in, hy_conv_w, hy_conv_b, hy_f_w1, hy_f_b1, hy_f_w2, hy_f_b2, hy_f_w3, hy_f_b3, hy_f_freq, hy_f_wout, hy_decay, hy_skip, hy_w_out, hy_b_out, na_w_qkv, na_b_qkv, na_rpb, na_w_o, na_b_o, ffn_w_gate, ffn_w_up, ffn_w_down):
    tabs = _fft_tables()
    row = lambda a: a.reshape(1, -1)
    m_tok = BATCH * SEQ

    s, x0 = _hyena_in(x, row(norm_mix[0]), hy_w_in[0].astype(BF16), row(hy_b_in[0]),
                      hy_conv_w[0], row(hy_conv_b[0]))
    col = lambda a: a.reshape(-1, 1)
    feat_t, t_col = _filter_features()
    w1t = jnp.zeros((HY_WIDTH, HY_EMB_PAD), F32).at[:, :HY_EMB_DIM].set(hy_f_w1[0].T)
    g_taps, g_sum = _hyena_filter(feat_t, t_col, w1t, col(hy_f_b1[0]), hy_f_w2[0].T, col(hy_f_b2[0]),
                                  hy_f_w3[0].T, col(hy_f_b3[0]), col(hy_f_freq[0]), hy_f_wout[0], hy_decay[0])
    bf = lambda name: jnp.asarray(tabs[name]).astype(BF16)
    ga = _fft_stage1(None, g_taps, tabs["cfr"], tabs["cfi"], tabs["twc"], tabs["tws"])
    sa = _fft_stage1(s, None, tabs["cr"], tabs["ci"], tabs["twc"], tabs["tws"])
    sb = _fft_stage2(sa, ga, bf("m2"), tabs["fr"], tabs["fi"], tabs["twc_t"], tabs["tws_t"])
    yv = _fft_stage3(sb, s, row(hy_skip[0]), g_sum, bf("m3"))
    xf = _proj_ffn(x, yv, x0, hy_w_out[0].astype(BF16), row(hy_b_out[0]),
                   row(norm_ffn[0]), ffn_w_gate[0].astype(BF16), ffn_w_up[0].astype(BF16),
                   ffn_w_down[0].astype(BF16)).reshape(m_tok, D_MODEL)

    q, k, v = _qkv(xf, row(norm_mix[1]), na_w_qkv[0].astype(BF16), row(na_b_qkv[0]))
    bias = _na_bias(na_rpb[0])
    blk = lambda a: a.reshape(BATCH, NA_NBLK, NA_QR * GRID_W, D_MODEL)
    att = _na_attention(blk(q), blk(k), blk(v), bias)
    xf = _proj_ffn_norm(xf, att.reshape(m_tok, D_MODEL), na_w_o[0].astype(BF16), row(na_b_o[0]),
                        row(norm_ffn[1]), ffn_w_gate[1].astype(BF16), ffn_w_up[1].astype(BF16),
                        ffn_w_down[1].astype(BF16), row(norm_final))
    return xf.reshape(BATCH, SEQ, D_MODEL)
```

```python
import functools
import math

import numpy as np
import jax
import jax.numpy as jnp
from jax import lax
from jax.experimental import pallas as pl
from jax.experimental.pallas import tpu as pltpu

F32 = jnp.float32
BF16 = jnp.bfloat16

D_MODEL = 1024
SEQ = 16384
BATCH = 2
FFN_HIDDEN = 2816
GRID_W = 64
RMS_EPS = 1e-6
HY_EMB_DIM = 33
HY_EMB_PAD = 40
HY_N_BANDS = 16
HY_WIDTH = 64
NA_HEADS = 16
NA_HEAD_DIM = 64
NA_KH = 8
NA_KW = 16

FFT_N = 2 * SEQ
FFT_N1 = 256
FFT_N2 = 128
FFT_H1 = FFT_N1 // 2
FFT_NB = 8
FFT_LANES = 512
N_SLABS = D_MODEL // 128
STEP_SLABS = FFT_LANES // 128

NA_ROWS = SEQ // GRID_W
NA_QR = 4
NA_KR = 12
NA_NBLK = NA_ROWS // NA_QR
NA_LOGITS_AHEAD = 2
NEG_BIG = -1e30
LOG2_E = math.log2(math.e)

VMEM_LIMIT = 56 * 1024 * 1024


def _cparams(sem):
    return pltpu.CompilerParams(dimension_semantics=sem, vmem_limit_bytes=VMEM_LIMIT)


def _const_spec(shape):
    nd = len(shape)
    return pl.BlockSpec(shape, lambda *_: (0,) * nd, pipeline_mode=pl.Buffered(1))


def _rms(x, g):
    ms = jnp.mean(x * x, axis=-1, keepdims=True)
    return x * lax.rsqrt(ms + RMS_EPS) * g


@functools.lru_cache(maxsize=None)
def _fft_tables():
    n1, n2, n = FFT_N1, FFT_N2, FFT_N
    k1 = np.arange(n1)[:, None]
    nh = np.arange(FFT_H1)[None, :]
    nf = np.arange(n1)[None, :]
    c = np.exp(-2j * np.pi * (nh * k1) / n1)
    cf = np.exp(-2j * np.pi * (nf * k1) / n1)
    e = np.exp(2j * np.pi * (nh * k1) / n1).T
    m3 = np.block([[e.real, -e.imag], [e.imag, e.real]])
    m3 = m3.reshape(2, FFT_H1, 2 * n1)
    a = np.arange(n2)
    f = np.exp(-2j * np.pi * np.outer(a, a) / n2)
    m2 = np.block([[f.real, -f.imag], [f.imag, f.real]])
    tw = np.exp(-2j * np.pi * np.outer(np.arange(n1), np.arange(n2)) / n)
    as_f = lambda m: np.asarray(m, dtype=np.float32)
    return dict(cr=as_f(c.real), ci=as_f(c.imag), cfr=as_f(cf.real), cfi=as_f(cf.imag), m3=as_f(m3),
                m2=as_f(m2), fr=as_f(f.real), fi=as_f(f.imag),
                twc=as_f(tw.real), tws=as_f(tw.imag),
                twc_t=as_f(tw.real.T), tws_t=as_f(tw.imag.T))


@functools.lru_cache(maxsize=None)
def _filter_features():
    ell = SEQ
    t = np.linspace(0.0, 1.0, ell, dtype=np.float32).astype(np.float64)
    w = (2.0 * math.pi * np.arange(ell, dtype=np.float32) / ell).astype(np.float32).astype(np.float64)
    bands = np.linspace(1e-4, HY_N_BANDS - 1, HY_N_BANDS, dtype=np.float32).astype(np.float64)
    ang = (bands[None, :] * w[:, None]).astype(np.float32).astype(np.float64)
    feat = np.concatenate([t[:, None], np.cos(ang), -np.sin(ang)], axis=-1)
    tab = np.zeros((2 * ell, HY_EMB_PAD), np.float64)
    tab[:ell, :HY_EMB_DIM] = feat
    tab[:ell, HY_EMB_DIM] = 1.0
    tab[ell + 1:, :HY_EMB_DIM] = feat[:0:-1]
    tab[ell + 1:, HY_EMB_DIM] = 1.0
    j, d, i, h = np.meshgrid(np.arange(FFT_N2 // FFT_NB), np.arange(2), np.arange(FFT_NB), np.arange(FFT_H1),
                             indexing="ij")
    perm = (FFT_N2 * (d * FFT_H1 + h) + FFT_NB * j + i).reshape(-1)
    tab = tab[perm]
    return np.asarray(tab.T, dtype=np.float32), np.asarray(tab[:, 0:1], dtype=np.float32)


def _na_bias_plan():
    plan = []
    for j in (0, 1, NA_NBLK - 1):
        base = NA_QR * min(max(j - 1, 0), NA_NBLK - 3)
        per_type = []
        for ri in range(NA_QR):
            r = NA_QR * j + ri
            rs = min(max(r - NA_KH // 2, 0), NA_ROWS - NA_KH)
            row = []
            for m in range(NA_KR // 2):
                kr = base + 2 * m
                row.append((kr - r + (NA_KH - 1), 0 <= kr - rs < NA_KH, 0 <= kr + 1 - rs < NA_KH))
            per_type.append(row)
        plan.append(per_type)
    return plan


def _na_bias_kernel(v_ref, o_ref):
    n_e = 2 * NA_KH
    c = lax.broadcasted_iota(jnp.int32, (GRID_W, 128), 0)
    lane = lax.broadcasted_iota(jnp.int32, (GRID_W, 128), 1)
    kc = lane % GRID_W
    cs = jnp.clip(c - NA_KW // 2, 0, GRID_W - NA_KW)
    col_ok = (kc >= cs) & (kc < cs + NA_KW)
    left = lane < GRID_W
    tiles = []
    for e in range(n_e):
        x = jnp.broadcast_to(v_ref[e:e + 1, :] * LOG2_E, (GRID_W, 128))
        tiles.append(pltpu.roll(x, 128 - (NA_KW - 1), 1, stride=1, stride_axis=0))
    neg = jnp.full((GRID_W, 128), NEG_BIG, F32)
    for t, per_type in enumerate(_na_bias_plan()):
        for ri, row in enumerate(per_type):
            for m, (ro_left, ok_l, ok_r) in enumerate(row):
                if not (ok_l or ok_r):
                    tile = neg
                else:
                    mask = col_ok if (ok_l and ok_r) else (col_ok & left if ok_l else col_ok & ~left)
                    tile = jnp.where(mask, tiles[ro_left + 1], NEG_BIG)
                o_ref[t, ri * GRID_W:(ri + 1) * GRID_W, m * 128:(m + 1) * 128] = tile


def _na_bias(rpb):
    n_e = 2 * NA_KH
    rp = jnp.pad(rpb, ((0, 0), (1, 1), (0, GRID_W - (2 * NA_KW - 1))))
    v = jnp.concatenate([rp[:, :n_e], rp[:, 1:n_e + 1]], axis=-1)
    tok, keys = NA_QR * GRID_W, NA_KR * GRID_W
    return pl.pallas_call(
        _na_bias_kernel, out_shape=jax.ShapeDtypeStruct((3, NA_HEADS, tok, keys), F32), grid=(NA_HEADS,),
        in_specs=[pl.BlockSpec((None, n_e, 128), lambda h: (h, 0, 0))],
        out_specs=pl.BlockSpec((3, None, tok, keys), lambda h: (0, h, 0, 0)),
        compiler_params=_cparams(("parallel",)), name="na_bias",
    )(v)


def _hy_in_kernel(xp_ref, x_ref, xn_ref, g_ref, w_ref, b_ref, cw_ref, cb_ref,
                  s_ref, x0_ref, z_ref, *, tl, nt):
    i = pl.program_id(1)
    xh = jnp.concatenate([xp_ref[...], x_ref[...], xn_ref[...]], axis=0)
    h = _rms(xh, g_ref[...]).astype(BF16)
    b = b_ref[...]
    row = lax.broadcasted_iota(jnp.int32, (tl + 16, 1), 0)
    inside = ((row >= 8) | (i > 0)) & ((row < tl + 8) | (i < nt - 1))
    z = jnp.where(inside, jnp.dot(h, w_ref[...], preferred_element_type=F32), -b)
    z_ref[0] = pltpu.roll(z, 1, 0)
    z_ref[1] = z
    z_ref[2] = pltpu.roll(z, tl + 15, 0)
    bias = cb_ref[...] + b * (cw_ref[0:1, :] + cw_ref[1:2, :] + cw_ref[2:3, :])

    def conv(c0):
        sl = slice(c0, c0 + D_MODEL)
        rows = slice(8, 8 + tl)
        return (cw_ref[0:1, sl] * z_ref[0, rows, sl] + cw_ref[1:2, sl] * z_ref[1, rows, sl]
                + cw_ref[2:3, sl] * z_ref[2, rows, sl] + bias[:, sl])

    x0_ref[...] = conv(0).astype(BF16)
    _store_grouped(s_ref, conv(2 * D_MODEL) * conv(D_MODEL))


def _store_grouped(ref, val):
    n1s = val.shape[0] // FFT_N2
    for c in range(N_SLABS):
        for jg in range(FFT_N2 // FFT_NB):
            tiles = [val[a * FFT_N2 + jg * FFT_NB:a * FFT_N2 + (jg + 1) * FFT_NB, c * 128:(c + 1) * 128]
                     for a in range(n1s)]
            ref[c, jg] = jnp.concatenate(tiles, axis=0)


def _load_grouped(ref):
    n1s = ref.shape[2] // FFT_NB
    slabs = []
    for c in range(N_SLABS):
        tiles = [ref[c, jg, a * FFT_NB:(a + 1) * FFT_NB, :] for a in range(n1s)
                 for jg in range(FFT_N2 // FFT_NB)]
        slabs.append(jnp.concatenate(tiles, axis=0))
    return jnp.concatenate(slabs, axis=1)


def _grouped_spec(tl):
    rows = tl // FFT_N2 * FFT_NB
    return pl.BlockSpec((None, N_SLABS, FFT_N2 // FFT_NB, rows, 128), lambda bb, i: (bb, 0, 0, i, 0))


def _hyena_in(x, g, w, b, cw, cb, tl=512):
    nt = SEQ // tl
    t8 = tl // 8
    kern = functools.partial(_hy_in_kernel, tl=tl, nt=nt)
    s_out = jax.ShapeDtypeStruct((BATCH, N_SLABS, FFT_N2 // FFT_NB, FFT_H1 * FFT_NB, 128), F32)
    x0_out = jax.ShapeDtypeStruct((BATCH, SEQ, D_MODEL), BF16)
    return pl.pallas_call(
        kern, out_shape=(s_out, x0_out), grid=(BATCH, nt),
        in_specs=[
            pl.BlockSpec((None, 8, D_MODEL), lambda bb, i: (bb, jnp.maximum(i * t8 - 1, 0), 0)),
            pl.BlockSpec((None, tl, D_MODEL), lambda bb, i: (bb, i, 0)),
            pl.BlockSpec((None, 8, D_MODEL), lambda bb, i: (bb, jnp.minimum((i + 1) * t8, SEQ // 8 - 1), 0)),
            _const_spec((1, D_MODEL)), _const_spec((D_MODEL, 3 * D_MODEL)), _const_spec((1, 3 * D_MODEL)),
            _const_spec((3, 3 * D_MODEL)), _const_spec((1, 3 * D_MODEL)),
        ],
        out_specs=(_grouped_spec(tl), pl.BlockSpec((None, tl, D_MODEL), lambda bb, i: (bb, i, 0))),
        scratch_shapes=[pltpu.VMEM((3, tl + 16, 3 * D_MODEL), F32)],
        compiler_params=_cparams(("parallel", "parallel")), name="hyena_in",
    )(x, x, x, g, w, b, cw, cb)


def _dot3(a, b, dims):
    a_hi = a.astype(BF16)
    a_lo = (a - a_hi.astype(F32)).astype(BF16)
    b_hi = b.astype(BF16)
    b_lo = (b - b_hi.astype(F32)).astype(BF16)
    dot = lambda u, v: lax.dot_general(u, v, dims, preferred_element_type=F32)
    return dot(a_hi, b_hi) + dot(a_hi, b_lo) + dot(a_lo, b_hi)


def _filter_kernel(ft_ref, tc_ref, w1_ref, b1_ref, w2_ref, b2_ref, w3_ref, b3_ref, fr_ref, wo_ref, dec_ref,
                   g_ref, sum_ref):
    hp = lax.Precision.HIGHEST
    ft = ft_ref[...]
    fr = fr_ref[...]
    h = jnp.sin(fr * (jnp.dot(w1_ref[...], ft, precision=hp, preferred_element_type=F32) + b1_ref[...]))
    h = jnp.sin(fr * (jnp.dot(w2_ref[...], h, precision=hp, preferred_element_type=F32) + b2_ref[...]))
    h = jnp.sin(fr * (jnp.dot(w3_ref[...], h, precision=hp, preferred_element_type=F32) + b3_ref[...]))
    h = h * ft[HY_EMB_DIM:HY_EMB_DIM + 1, :]
    half = FFT_NB * FFT_H1
    total = jnp.zeros((1, D_MODEL), F32)
    for d in range(2):
        hd = h[:, d * half:(d + 1) * half]
        o = _dot3(hd, wo_ref[:, d * D_MODEL:(d + 1) * D_MODEL], (((0,), (0,)), ((), ())))
        t = tc_ref[d * half:(d + 1) * half, :]
        gq = o * jnp.exp(-t * jnp.abs(dec_ref[d:d + 1, :]))
        total = total + jnp.sum(jnp.abs(gq), axis=0, keepdims=True)
        for i in range(FFT_NB):
            g_ref[i, d * FFT_H1:(d + 1) * FFT_H1, :] = gq[i * FFT_H1:(i + 1) * FFT_H1].astype(BF16)

    @pl.when(pl.program_id(0) == 0)
    def _():
        sum_ref[...] = jnp.zeros_like(sum_ref)

    sum_ref[...] += total


def _hyena_filter(feat_t, t_col, w1t, b1, w2t, b2, w3t, b3, freq, wout, decay):
    tr = 2 * FFT_NB * FFT_H1
    col = lambda: _const_spec((HY_WIDTH, 1))
    return pl.pallas_call(
        _filter_kernel,
        out_shape=(jax.ShapeDtypeStruct((FFT_N2, FFT_N1, D_MODEL), BF16),
                   jax.ShapeDtypeStruct((1, D_MODEL), F32)),
        grid=(FFT_N2 // FFT_NB,),
        in_specs=[
            pl.BlockSpec((HY_EMB_PAD, tr), lambda j: (0, j)),
            pl.BlockSpec((tr, 1), lambda j: (j, 0)),
            _const_spec((HY_WIDTH, HY_EMB_PAD)), col(),
            _const_spec((HY_WIDTH, HY_WIDTH)), col(),
            _const_spec((HY_WIDTH, HY_WIDTH)), col(),
            col(),
            _const_spec((HY_WIDTH, 2 * D_MODEL)),
            _const_spec((2, D_MODEL)),
        ],
        out_specs=(pl.BlockSpec((FFT_NB, FFT_N1, D_MODEL), lambda j: (j, 0, 0)),
                   pl.BlockSpec((1, D_MODEL), lambda j: (0, 0))),
        compiler_params=_cparams(("arbitrary",)), name="hyena_filter",
    )(feat_t, t_col, w1t, b1, w2t, b2, w3t, b3, freq, wout, decay)


def _select_column(tab_ref, idx):
    tab = tab_ref[...]
    lane = lax.broadcasted_iota(jnp.int32, tab.shape, 1)
    return jnp.sum(jnp.where(lane == idx, tab, 0.0), axis=-1, keepdims=True)


def _stack_bf16(re, im):
    return jnp.concatenate([re, im], axis=0).astype(BF16)


def _twiddled(cr, ci, c, s):
    return cr * c - ci * s, cr * s + ci * c


def _gather_rows(ref, lead, i, rows):
    slabs = ref.shape[len(lead)]
    parts = [ref[(*lead, c, pl.ds(i, rows, stride=FFT_NB), slice(None))] for c in range(slabs)]
    return jnp.concatenate(parts, axis=1)


def _scatter_rows(ref, i, val):
    rows = val.shape[0]
    for c in range(ref.shape[0]):
        ref[c, pl.ds(i, rows, stride=FFT_NB), :] = val[:, c * 128:(c + 1) * 128]


def _tile_rows(ref, q):
    slabs, groups = ref.shape[:2]
    parts = [ref[c, :, q * FFT_NB:(q + 1) * FFT_NB, :].reshape(groups * FFT_NB, 128) for c in range(slabs)]
    return jnp.concatenate(parts, axis=1)


def _fft1_kernel(s_ref, cr_ref, ci_ref, twc_ref, tws_ref, ar_ref, ai_ref):
    n2_0 = pl.program_id(1) * FFT_NB
    for i in range(FFT_NB):
        pr, pi = _twiddled(cr_ref[...], ci_ref[...],
                           _select_column(twc_ref, n2_0 + i), _select_column(tws_ref, n2_0 + i))
        m = jnp.concatenate([jnp.concatenate([pr, -pi], axis=1), jnp.concatenate([pi, pr], axis=1)], axis=0)
        rhs = _stack_bf16(_gather_rows(s_ref, (0,), i, FFT_H1), _gather_rows(s_ref, (1,), i, FFT_H1))
        a = jnp.dot(m.astype(BF16), rhs, preferred_element_type=F32)
        _scatter_rows(ar_ref, i, a[:FFT_N1])
        _scatter_rows(ai_ref, i, a[FFT_N1:])


def _fft1_filter_kernel(g_ref, cr_ref, ci_ref, twc_ref, tws_ref, ar_ref, ai_ref):
    n2_0 = pl.program_id(1) * FFT_NB
    for i in range(FFT_NB):
        pr, pi = _twiddled(cr_ref[...], ci_ref[...],
                           _select_column(twc_ref, n2_0 + i), _select_column(tws_ref, n2_0 + i))
        m = jnp.concatenate([pr, pi], axis=0).astype(BF16)
        a = jnp.dot(m, g_ref[i], preferred_element_type=F32)
        _scatter_rows(ar_ref, i, a[:FFT_N1])
        _scatter_rows(ai_ref, i, a[FFT_N1:])


def _fft_stage1(s5, g3, cr, ci, twc, tws):
    consts = [_const_spec(cr.shape), _const_spec(ci.shape),
              _const_spec((FFT_N1, FFT_N2)), _const_spec((FFT_N1, FFT_N2))]
    out = jax.ShapeDtypeStruct((N_SLABS, FFT_N2 // FFT_NB, FFT_N1 * FFT_NB, 128), F32)
    oblk = pl.BlockSpec((STEP_SLABS, None, FFT_N1 * FFT_NB, 128), lambda h, j: (h, j, 0, 0))
    common = dict(out_shape=(out, out), grid=(D_MODEL // FFT_LANES, FFT_N2 // FFT_NB), out_specs=(oblk, oblk),
                  compiler_params=_cparams(("parallel", "parallel")))
    if g3 is None:
        return pl.pallas_call(
            _fft1_kernel,
            in_specs=[pl.BlockSpec((2, STEP_SLABS, None, FFT_H1 * FFT_NB, 128),
                                   lambda h, j: (0, h, j, 0, 0))] + consts,
            name="fft_stage1", **common,
        )(s5, cr, ci, twc, tws)
    return pl.pallas_call(
        _fft1_filter_kernel,
        in_specs=[pl.BlockSpec((FFT_NB, FFT_N1, FFT_LANES), lambda h, j: (j, 0, h))] + consts,
        name="fft_stage1_filter", **common,
    )(g3, cr, ci, twc, tws)


def _fft2_kernel(ar_ref, ai_ref, gr_ref, gi_ref, m_ref, fr_ref, fi_ref, twc_ref, tws_ref, br_ref, bi_ref):
    k0 = pl.program_id(1) * FFT_NB
    lanes = ar_ref.shape[0] * 128
    rhs = jnp.concatenate(
        [_stack_bf16(_tile_rows(r, q), _tile_rows(i, q))
         for q in range(FFT_NB) for r, i in ((ar_ref, ai_ref), (gr_ref, gi_ref))], axis=1)
    zg = jnp.dot(m_ref[...], rhs, preferred_element_type=F32)
    for q in range(FFT_NB):
        z = zg[:, 2 * q * lanes:(2 * q + 1) * lanes]
        g = zg[:, (2 * q + 1) * lanes:(2 * q + 2) * lanes]
        zr, zi = z[:FFT_N2], z[FFT_N2:]
        gr, gi = g[:FFT_N2], g[FFT_N2:]
        y = _stack_bf16(zr * gr - zi * gi, zr * gi + zi * gr)
        pr, pi = _twiddled(fr_ref[...], -fi_ref[...],
                           _select_column(twc_ref, k0 + q), -_select_column(tws_ref, k0 + q))
        mc = jnp.concatenate([jnp.concatenate([pr, -pi], axis=1), jnp.concatenate([pi, pr], axis=1)], axis=0)
        bb = jnp.dot(mc.astype(BF16), y, preferred_element_type=F32)
        _scatter_rows(br_ref, q, bb[:FFT_N2])
        _scatter_rows(bi_ref, q, bb[FFT_N2:])


def _fft_stage2(a, ag, m2, fr, fi, twc_t, tws_t):
    blk = pl.BlockSpec((STEP_SLABS, FFT_N2 // FFT_NB, FFT_NB * FFT_NB, 128), lambda h, k: (h, 0, k, 0))
    out = jax.ShapeDtypeStruct((N_SLABS, FFT_N1 // FFT_NB, FFT_N2 * FFT_NB, 128), F32)
    oblk = pl.BlockSpec((STEP_SLABS, None, FFT_N2 * FFT_NB, 128), lambda h, k: (h, k, 0, 0))
    return pl.pallas_call(
        _fft2_kernel, out_shape=(out, out), grid=(D_MODEL // FFT_LANES, FFT_N1 // FFT_NB),
        in_specs=[blk, blk, blk, blk, _const_spec((2 * FFT_N2, 2 * FFT_N2)),
                  _const_spec((FFT_N2, FFT_N2)), _const_spec((FFT_N2, FFT_N2)),
                  _const_spec((FFT_N2, FFT_N1)), _const_spec((FFT_N2, FFT_N1))],
        out_specs=(oblk, oblk),
        compiler_params=_cparams(("parallel", "parallel")), name="fft_stage2",
    )(*a, *ag, m2, fr, fi, twc_t, tws_t)


def _fft3_kernel(br_ref, bi_ref, s_ref, skip_ref, sum_ref, m_ref, y_ref):
    skip = skip_ref[...]
    scale = 1.0 / (sum_ref[...] * FFT_N)
    for i in range(FFT_NB):
        rhs = _stack_bf16(_tile_rows(br_ref, i), _tile_rows(bi_ref, i))
        y = jnp.dot(m_ref[...], rhs, preferred_element_type=F32)
        _scatter_rows(y_ref, i, y * scale + _gather_rows(s_ref, (), i, FFT_H1) * skip)


def _fft_stage3(b, s5, skip, gsum, m3):
    act = pl.BlockSpec((None, STEP_SLABS, None, FFT_H1 * FFT_NB, 128), lambda h, j, bb: (bb, h, j, 0, 0))
    bblk = pl.BlockSpec((STEP_SLABS, FFT_N1 // FFT_NB, FFT_NB * FFT_NB, 128), lambda h, j, bb: (h, 0, j, 0))
    vec = pl.BlockSpec((1, FFT_LANES), lambda h, j, bb: (0, h))
    return pl.pallas_call(
        _fft3_kernel, out_shape=jax.ShapeDtypeStruct(s5.shape, F32),
        grid=(D_MODEL // FFT_LANES, FFT_N2 // FFT_NB, BATCH),
        in_specs=[bblk, bblk, act, vec, vec,
                  pl.BlockSpec((None, FFT_H1, 2 * FFT_N1), lambda h, j, bb: (bb, 0, 0))],
        out_specs=act,
        compiler_params=_cparams(("parallel", "parallel", "arbitrary")), name="fft_stage3",
    )(*b, s5, skip, gsum, m3)


def _swiglu_residual(x, g, wg_ref, wu_ref, wd_ref):
    h = _rms(x, g).astype(BF16)
    gate = jnp.dot(h, wg_ref[...], preferred_element_type=F32)
    up = jnp.dot(h, wu_ref[...], preferred_element_type=F32)
    a = (gate * jax.nn.sigmoid(gate) * up).astype(BF16)
    return x + jnp.dot(a, wd_ref[...], preferred_element_type=F32)


def _proj_ffn_kernel(x_ref, y_ref, x0_ref, wo_ref, bo_ref, g_ref, wg_ref, wu_ref, wd_ref, o_ref):
    y = (_load_grouped(y_ref) * x0_ref[...].astype(F32)).astype(BF16)
    x = x_ref[...] + jnp.dot(y, wo_ref[...], preferred_element_type=F32) + bo_ref[...]
    o_ref[...] = _swiglu_residual(x, g_ref[...], wg_ref, wu_ref, wd_ref)


def _proj_ffn_norm_kernel(x_ref, y_ref, wo_ref, bo_ref, g_ref, wg_ref, wu_ref, wd_ref, gf_ref, o_ref):
    x = x_ref[...] + jnp.dot(y_ref[...], wo_ref[...], preferred_element_type=F32) + bo_ref[...]
    o_ref[...] = _rms(_swiglu_residual(x, g_ref[...], wg_ref, wu_ref, wd_ref), gf_ref[...])


def _proj_ffn_weight_specs():
    return [_const_spec((D_MODEL, D_MODEL)), _const_spec((1, D_MODEL)), _const_spec((1, D_MODEL)),
            _const_spec((D_MODEL, FFN_HIDDEN)), _const_spec((D_MODEL, FFN_HIDDEN)),
            _const_spec((FFN_HIDDEN, D_MODEL))]


def _proj_ffn(x3, y5, x0, wo, bo, g, wg, wu, wd, tm=512):
    blk = pl.BlockSpec((None, tm, D_MODEL), lambda bb, i: (bb, i, 0))
    return pl.pallas_call(
        _proj_ffn_kernel, out_shape=jax.ShapeDtypeStruct(x3.shape, F32), grid=(BATCH, SEQ // tm),
        in_specs=[blk, _grouped_spec(tm), blk] + _proj_ffn_weight_specs(),
        out_specs=blk,
        compiler_params=_cparams(("parallel", "parallel")), name="proj_ffn",
    )(x3, y5, x0, wo, bo, g, wg, wu, wd)


def _proj_ffn_norm(x, y, wo, bo, g, wg, wu, wd, gf, tm=512):
    m = x.shape[0]
    blk = pl.BlockSpec((tm, D_MODEL), lambda i: (i, 0))
    return pl.pallas_call(
        _proj_ffn_norm_kernel, out_shape=jax.ShapeDtypeStruct((m, D_MODEL), F32), grid=(m // tm,),
        in_specs=[blk, blk] + _proj_ffn_weight_specs() + [_const_spec((1, D_MODEL))],
        out_specs=blk,
        compiler_params=_cparams(("parallel",)), name="proj_ffn_norm",
    )(x, y, wo, bo, g, wg, wu, wd, gf)


def _qkv_kernel(x_ref, g_ref, w_ref, b_ref, q_ref, k_ref, v_ref):
    h = _rms(x_ref[...], g_ref[...]).astype(BF16)
    z = jnp.dot(h, w_ref[...], preferred_element_type=F32) + b_ref[...]
    q_ref[...] = (z[:, :D_MODEL] * (NA_HEAD_DIM ** -0.5 * LOG2_E)).astype(BF16)
    k_ref[...] = z[:, D_MODEL:2 * D_MODEL].astype(BF16)
    v_ref[...] = z[:, 2 * D_MODEL:].astype(BF16)


def _qkv(x, g, w, b, tm=1024):
    m = x.shape[0]
    act = jax.ShapeDtypeStruct((m, D_MODEL), BF16)
    blk = pl.BlockSpec((tm, D_MODEL), lambda i: (i, 0))
    return pl.pallas_call(
        _qkv_kernel, out_shape=(act, act, act), grid=(m // tm,),
        in_specs=[blk, _const_spec((1, D_MODEL)), _const_spec((D_MODEL, 3 * D_MODEL)),
                  _const_spec((1, 3 * D_MODEL))],
        out_specs=(blk, blk, blk),
        compiler_params=_cparams(("parallel",)), name="na_qkv",
    )(x, g, w, b)


def _na_kernel(q_ref, k0_ref, k1_ref, k2_ref, v0_ref, v1_ref, v2_ref, bias_ref, o_ref):
    tok = NA_QR * GRID_W
    lane = lax.broadcasted_iota(jnp.int32, (1, 128), 1)
    first = lane < NA_HEAD_DIM
    ones = jnp.ones((NA_KR * GRID_W, 128), BF16)
    def logits(p):
        sl = slice(128 * p, 128 * (p + 1))
        qp = q_ref[:, sl]
        zero = jnp.zeros_like(qp)
        q2 = jnp.concatenate([jnp.where(first, qp, zero), jnp.where(first, zero, qp)], axis=0)
        kp = jnp.concatenate([k0_ref[:, sl], k1_ref[:, sl], k2_ref[:, sl]], axis=0)
        s = lax.dot_general(q2, kp, (((1,), (1,)), ((), ())), preferred_element_type=F32)
        return s + bias_ref[p]

    pending = [logits(p) for p in range(NA_LOGITS_AHEAD)]
    for p in range(NA_HEADS // 2):
        sl = slice(128 * p, 128 * (p + 1))
        s = pending.pop(0)
        if p + NA_LOGITS_AHEAD < NA_HEADS // 2:
            pending.append(logits(p + NA_LOGITS_AHEAD))
        vp = jnp.concatenate([v0_ref[:, sl], v1_ref[:, sl], v2_ref[:, sl]], axis=0)
        e = jnp.exp2(s - jnp.max(s, axis=-1, keepdims=True)).astype(BF16)
        o = jnp.dot(e, jnp.concatenate([vp, ones], axis=1), preferred_element_type=F32)
        o = o[:, :128] * (1.0 / o[:, 128:])
        o_ref[:, sl] = jnp.where(first, o[:tok], o[tok:]).astype(BF16)


def _na_attention(q, k, v, bias):
    tok = NA_QR * GRID_W
    bias = bias.reshape(3, NA_HEADS // 2, 2 * tok, NA_KR * GRID_W)

    def chunk(t):
        return pl.BlockSpec((None, None, tok, D_MODEL),
                            lambda bb, j: (bb, jnp.clip(j - 1, 0, NA_NBLK - 3) + t, 0, 0))

    def btype(bb, j):
        return (jnp.where(j == 0, 0, jnp.where(j == NA_NBLK - 1, 2, 1)), 0, 0, 0)

    qspec = pl.BlockSpec((None, None, tok, D_MODEL), lambda bb, j: (bb, j, 0, 0))
    return pl.pallas_call(
        _na_kernel, out_shape=jax.ShapeDtypeStruct(q.shape, BF16), grid=(BATCH, NA_NBLK),
        in_specs=[qspec, chunk(0), chunk(1), chunk(2), chunk(0), chunk(1), chunk(2),
                  pl.BlockSpec((None, NA_HEADS // 2, 2 * tok, NA_KR * GRID_W), btype,
                               pipeline_mode=pl.Buffered(1))],
        out_specs=qspec,
        compiler_params=_cparams(("parallel", "arbitrary")), name="na_attention",
    )(q, k, k, k, v, v, v, bias)


def kernel(x, norm_mix, norm_ffn, norm_final, hy_w_in, hy_b_in, hy_conv_w, hy_conv_b, hy_f_w1, hy_f_b1, hy_f_w2, hy_f_b2, hy_f_w3, hy_f_b3, hy_f_freq, hy_f_wout, hy_decay, hy_skip, hy_w_out, hy_b_out, na_w_qkv, na_b_qkv, na_rpb, na_w_o, na_b_o, ffn_w_gate, ffn_w_up, ffn_w_down):
    tabs = _fft_tables()
    row = lambda a: a.reshape(1, -1)
    m_tok = BATCH * SEQ

    s, x0 = _hyena_in(x, row(norm_mix[0]), hy_w_in[0].astype(BF16), row(hy_b_in[0]),
                      hy_conv_w[0], row(hy_conv_b[0]))
    col = lambda a: a.reshape(-1, 1)
    feat_t, t_col = _filter_features()
    w1t = jnp.zeros((HY_WIDTH, HY_EMB_PAD), F32).at[:, :HY_EMB_DIM].set(hy_f_w1[0].T)
    g_taps, g_sum = _hyena_filter(feat_t, t_col, w1t, col(hy_f_b1[0]), hy_f_w2[0].T, col(hy_f_b2[0]),
                                  hy_f_w3[0].T, col(hy_f_b3[0]), col(hy_f_freq[0]), hy_f_wout[0], hy_decay[0])
    bf = lambda name: jnp.asarray(tabs[name]).astype(BF16)
    ga = _fft_stage1(None, g_taps, tabs["cfr"], tabs["cfi"], tabs["twc"], tabs["tws"])
    sa = _fft_stage1(s, None, tabs["cr"], tabs["ci"], tabs["twc"], tabs["tws"])
    sb = _fft_stage2(sa, ga, bf("m2"), tabs["fr"], tabs["fi"], tabs["twc_t"], tabs["tws_t"])
    yv = _fft_stage3(sb, s, row(hy_skip[0]), g_sum, bf("m3"))
    xf = _proj_ffn(x, yv, x0, hy_w_out[0].astype(BF16), row(hy_b_out[0]),
                   row(norm_ffn[0]), ffn_w_gate[0].astype(BF16), ffn_w_up[0].astype(BF16),
                   ffn_w_down[0].astype(BF16)).reshape(m_tok, D_MODEL)

    q, k, v = _qkv(xf, row(norm_mix[1]), na_w_qkv[0].astype(BF16), row(na_b_qkv[0]))
    bias = _na_bias(na_rpb[0])
    blk = lambda a: a.reshape(BATCH, NA_NBLK, NA_QR * GRID_W, D_MODEL)
    att = _na_attention(blk(q), blk(k), blk(v), bias)
    xf = _proj_ffn_norm(xf, att.reshape(m_tok, D_MODEL), na_w_o[0].astype(BF16), row(na_b_o[0]),
                        row(norm_ffn[1]), ffn_w_gate[1].astype(BF16), ffn_w_up[1].astype(BF16),
                        ffn_w_down[1].astype(BF16), row(norm_final))
    return xf.reshape(BATCH, SEQ, D_MODEL)
```

```python
import functools
import math

import numpy as np
import jax
import jax.numpy as jnp
from jax import lax
from jax.experimental import pallas as pl
from jax.experimental.pallas import tpu as pltpu

F32 = jnp.float32
BF16 = jnp.bfloat16

D_MODEL = 1024
SEQ = 16384
BATCH = 2
FFN_HIDDEN = 2816
GRID_W = 64
RMS_EPS = 1e-6
HY_EMB_DIM = 33
HY_EMB_PAD = 40
HY_N_BANDS = 16
HY_WIDTH = 64
NA_HEADS = 16
NA_HEAD_DIM = 64
NA_KH = 8
NA_KW = 16

FFT_N = 2 * SEQ
FFT_N1 = 256
FFT_N2 = 128
FFT_H1 = FFT_N1 // 2
FFT_NB = 8
FFT_LANES = 512
N_SLABS = D_MODEL // 128
STEP_SLABS = FFT_LANES // 128

NA_ROWS = SEQ // GRID_W
NA_QR = 4
NA_KR = 12
NA_NBLK = NA_ROWS // NA_QR
NA_LOGITS_AHEAD = 2
NEG_BIG = -1e30
LOG2_E = math.log2(math.e)

VMEM_LIMIT = 56 * 1024 * 1024


def _cparams(sem):
    return pltpu.CompilerParams(dimension_semantics=sem, vmem_limit_bytes=VMEM_LIMIT)


def _const_spec(shape):
    nd = len(shape)
    return pl.BlockSpec(shape, lambda *_: (0,) * nd, pipeline_mode=pl.Buffered(1))


def _rms(x, g):
    ms = jnp.mean(x * x, axis=-1, keepdims=True)
    return x * lax.rsqrt(ms + RMS_EPS) * g


@functools.lru_cache(maxsize=None)
def _fft_tables():
    n1, n2, n = FFT_N1, FFT_N2, FFT_N
    k1 = np.arange(n1)[:, None]
    nh = np.arange(FFT_H1)[None, :]
    nf = np.arange(n1)[None, :]
    c = np.exp(-2j * np.pi * (nh * k1) / n1)
    cf = np.exp(-2j * np.pi * (nf * k1) / n1)
    e = np.exp(2j * np.pi * (nh * k1) / n1).T
    m3 = np.block([[e.real, -e.imag], [e.imag, e.real]])
    a = np.arange(n2)
    f = np.exp(-2j * np.pi * np.outer(a, a) / n2)
    m2 = np.block([[f.real, -f.imag], [f.imag, f.real]])
    tw = np.exp(-2j * np.pi * np.outer(np.arange(n1), np.arange(n2)) / n)
    as_f = lambda m: np.asarray(m, dtype=np.float32)
    return dict(cr=as_f(c.real), ci=as_f(c.imag), cfr=as_f(cf.real), cfi=as_f(cf.imag), m3=as_f(m3),
                m2=as_f(m2), fr=as_f(f.real), fi=as_f(f.imag),
                twc=as_f(tw.real), tws=as_f(tw.imag),
                twc_t=as_f(tw.real.T), tws_t=as_f(tw.imag.T))


@functools.lru_cache(maxsize=None)
def _filter_features():
    ell = SEQ
    t = np.linspace(0.0, 1.0, ell, dtype=np.float32).astype(np.float64)
    w = (2.0 * math.pi * np.arange(ell, dtype=np.float32) / ell).astype(np.float32).astype(np.float64)
    bands = np.linspace(1e-4, HY_N_BANDS - 1, HY_N_BANDS, dtype=np.float32).astype(np.float64)
    ang = (bands[None, :] * w[:, None]).astype(np.float32).astype(np.float64)
    feat = np.concatenate([t[:, None], np.cos(ang), -np.sin(ang)], axis=-1)
    tab = np.zeros((2 * ell, HY_EMB_PAD), np.float64)
    tab[:ell, :HY_EMB_DIM] = feat
    tab[:ell, HY_EMB_DIM] = 1.0
    tab[ell + 1:, :HY_EMB_DIM] = feat[:0:-1]
    tab[ell + 1:, HY_EMB_DIM] = 1.0
    j, d, i, h = np.meshgrid(np.arange(FFT_N2 // FFT_NB), np.arange(2), np.arange(FFT_NB), np.arange(FFT_H1),
                             indexing="ij")
    perm = (FFT_N2 * (d * FFT_H1 + h) + FFT_NB * j + i).reshape(-1)
    tab = tab[perm]
    return np.asarray(tab.T, dtype=np.float32), np.asarray(tab[:, 0:1], dtype=np.float32)


def _na_bias_plan():
    plan = []
    for j in (0, 1, NA_NBLK - 1):
        base = NA_QR * min(max(j - 1, 0), NA_NBLK - 3)
        per_type = []
        for ri in range(NA_QR):
            r = NA_QR * j + ri
            rs = min(max(r - NA_KH // 2, 0), NA_ROWS - NA_KH)
            row = []
            for m in range(NA_KR // 2):
                kr = base + 2 * m
                row.append((kr - r + (NA_KH - 1), 0 <= kr - rs < NA_KH, 0 <= kr + 1 - rs < NA_KH))
            per_type.append(row)
        plan.append(per_type)
    return plan


def _na_bias_kernel(v_ref, o_ref):
    n_e = 2 * NA_KH
    c = lax.broadcasted_iota(jnp.int32, (GRID_W, 128), 0)
    lane = lax.broadcasted_iota(jnp.int32, (GRID_W, 128), 1)
    kc = lane % GRID_W
    cs = jnp.clip(c - NA_KW // 2, 0, GRID_W - NA_KW)
    col_ok = (kc >= cs) & (kc < cs + NA_KW)
    left = lane < GRID_W
    tiles = []
    for e in range(n_e):
        x = jnp.broadcast_to(v_ref[e:e + 1, :] * LOG2_E, (GRID_W, 128))
        tiles.append(pltpu.roll(x, 128 - (NA_KW - 1), 1, stride=1, stride_axis=0))
    neg = jnp.full((GRID_W, 128), NEG_BIG, F32)
    for t, per_type in enumerate(_na_bias_plan()):
        for ri, row in enumerate(per_type):
            for m, (ro_left, ok_l, ok_r) in enumerate(row):
                if not (ok_l or ok_r):
                    tile = neg
                else:
                    mask = col_ok if (ok_l and ok_r) else (col_ok & left if ok_l else col_ok & ~left)
                    tile = jnp.where(mask, tiles[ro_left + 1], NEG_BIG)
                o_ref[t, ri * GRID_W:(ri + 1) * GRID_W, m * 128:(m + 1) * 128] = tile


def _na_bias(rpb):
    n_e = 2 * NA_KH
    rp = jnp.pad(rpb, ((0, 0), (1, 1), (0, GRID_W - (2 * NA_KW - 1))))
    v = jnp.concatenate([rp[:, :n_e], rp[:, 1:n_e + 1]], axis=-1)
    tok, keys = NA_QR * GRID_W, NA_KR * GRID_W
    return pl.pallas_call(
        _na_bias_kernel, out_shape=jax.ShapeDtypeStruct((3, NA_HEADS, tok, keys), F32), grid=(NA_HEADS,),
        in_specs=[pl.BlockSpec((None, n_e, 128), lambda h: (h, 0, 0))],
        out_specs=pl.BlockSpec((3, None, tok, keys), lambda h: (0, h, 0, 0)),
        compiler_params=_cparams(("parallel",)), name="na_bias",
    )(v)


def _hy_in_kernel(xp_ref, x_ref, xn_ref, g_ref, w_ref, b_ref, cw_ref, cb_ref,
                  s_ref, x0_ref, z_ref, *, tl, nt):
    i = pl.program_id(1)
    xh = jnp.concatenate([xp_ref[...], x_ref[...], xn_ref[...]], axis=0)
    h = _rms(xh, g_ref[...]).astype(BF16)
    b = b_ref[...]
    row = lax.broadcasted_iota(jnp.int32, (tl + 16, 1), 0)
    inside = ((row >= 8) | (i > 0)) & ((row < tl + 8) | (i < nt - 1))
    z = jnp.where(inside, jnp.dot(h, w_ref[...], preferred_element_type=F32), -b)
    z_ref[0] = pltpu.roll(z, 1, 0)
    z_ref[1] = z
    z_ref[2] = pltpu.roll(z, tl + 15, 0)
    bias = cb_ref[...] + b * (cw_ref[0:1, :] + cw_ref[1:2, :] + cw_ref[2:3, :])

    def conv(c0):
        sl = slice(c0, c0 + D_MODEL)
        rows = slice(8, 8 + tl)
        return (cw_ref[0:1, sl] * z_ref[0, rows, sl] + cw_ref[1:2, sl] * z_ref[1, rows, sl]
                + cw_ref[2:3, sl] * z_ref[2, rows, sl] + bias[:, sl])

    x0_ref[...] = conv(0).astype(BF16)
    _store_grouped(s_ref, conv(2 * D_MODEL) * conv(D_MODEL))


def _store_grouped(ref, val):
    n1s = val.shape[0] // FFT_N2
    for c in range(N_SLABS):
        for jg in range(FFT_N2 // FFT_NB):
            tiles = [val[a * FFT_N2 + jg * FFT_NB:a * FFT_N2 + (jg + 1) * FFT_NB, c * 128:(c + 1) * 128]
                     for a in range(n1s)]
            ref[c, jg] = jnp.concatenate(tiles, axis=0)


def _load_grouped(ref):
    n1s = ref.shape[2] // FFT_NB
    slabs = []
    for c in range(N_SLABS):
        tiles = [ref[c, jg, a * FFT_NB:(a + 1) * FFT_NB, :] for a in range(n1s)
                 for jg in range(FFT_N2 // FFT_NB)]
        slabs.append(jnp.concatenate(tiles, axis=0))
    return jnp.concatenate(slabs, axis=1)


def _grouped_spec(tl):
    rows = tl // FFT_N2 * FFT_NB
    return pl.BlockSpec((None, N_SLABS, FFT_N2 // FFT_NB, rows, 128), lambda bb, i: (bb, 0, 0, i, 0))


def _hyena_in(x, g, w, b, cw, cb, tl=512):
    nt = SEQ // tl
    t8 = tl // 8
    kern = functools.partial(_hy_in_kernel, tl=tl, nt=nt)
    s_out = jax.ShapeDtypeStruct((BATCH, N_SLABS, FFT_N2 // FFT_NB, FFT_H1 * FFT_NB, 128), F32)
    x0_out = jax.ShapeDtypeStruct((BATCH, SEQ, D_MODEL), BF16)
    return pl.pallas_call(
        kern, out_shape=(s_out, x0_out), grid=(BATCH, nt),
        in_specs=[
            pl.BlockSpec((None, 8, D_MODEL), lambda bb, i: (bb, jnp.maximum(i * t8 - 1, 0), 0)),
            pl.BlockSpec((None, tl, D_MODEL), lambda bb, i: (bb, i, 0)),
            pl.BlockSpec((None, 8, D_MODEL), lambda bb, i: (bb, jnp.minimum((i + 1) * t8, SEQ // 8 - 1), 0)),
            _const_spec((1, D_MODEL)), _const_spec((D_MODEL, 3 * D_MODEL)), _const_spec((1, 3 * D_MODEL)),
            _const_spec((3, 3 * D_MODEL)), _const_spec((1, 3 * D_MODEL)),
        ],
        out_specs=(_grouped_spec(tl), pl.BlockSpec((None, tl, D_MODEL), lambda bb, i: (bb, i, 0))),
        scratch_shapes=[pltpu.VMEM((3, tl + 16, 3 * D_MODEL), F32)],
        compiler_params=_cparams(("parallel", "parallel")), name="hyena_in",
    )(x, x, x, g, w, b, cw, cb)


def _dot3(a, b, dims):
    a_hi = a.astype(BF16)
    a_lo = (a - a_hi.astype(F32)).astype(BF16)
    b_hi = b.astype(BF16)
    b_lo = (b - b_hi.astype(F32)).astype(BF16)
    dot = lambda u, v: lax.dot_general(u, v, dims, preferred_element_type=F32)
    return dot(a_hi, b_hi) + dot(a_hi, b_lo) + dot(a_lo, b_hi)


def _filter_kernel(ft_ref, tc_ref, w1_ref, b1_ref, w2_ref, b2_ref, w3_ref, b3_ref, fr_ref, wo_ref, dec_ref,
                   g_ref, sum_ref):
    hp = lax.Precision.HIGHEST
    ft = ft_ref[...]
    fr = fr_ref[...]
    h = jnp.sin(fr * (jnp.dot(w1_ref[...], ft, precision=hp, preferred_element_type=F32) + b1_ref[...]))
    h = jnp.sin(fr * (jnp.dot(w2_ref[...], h, precision=hp, preferred_element_type=F32) + b2_ref[...]))
    h = jnp.sin(fr * (jnp.dot(w3_ref[...], h, precision=hp, preferred_element_type=F32) + b3_ref[...]))
    h = h * ft[HY_EMB_DIM:HY_EMB_DIM + 1, :]
    half = FFT_NB * FFT_H1
    total = jnp.zeros((1, D_MODEL), F32)
    for d in range(2):
        hd = h[:, d * half:(d + 1) * half]
        o = _dot3(hd, wo_ref[:, d * D_MODEL:(d + 1) * D_MODEL], (((0,), (0,)), ((), ())))
        t = tc_ref[d * half:(d + 1) * half, :]
        gq = o * jnp.exp(-t * jnp.abs(dec_ref[d:d + 1, :]))
        total = total + jnp.sum(jnp.abs(gq), axis=0, keepdims=True)
        for i in range(FFT_NB):
            g_ref[i, d * FFT_H1:(d + 1) * FFT_H1, :] = gq[i * FFT_H1:(i + 1) * FFT_H1].astype(BF16)

    @pl.when(pl.program_id(0) == 0)
    def _():
        sum_ref[...] = jnp.zeros_like(sum_ref)

    sum_ref[...] += total


def _hyena_filter(feat_t, t_col, w1t, b1, w2t, b2, w3t, b3, freq, wout, decay):
    tr = 2 * FFT_NB * FFT_H1
    col = lambda: _const_spec((HY_WIDTH, 1))
    return pl.pallas_call(
        _filter_kernel,
        out_shape=(jax.ShapeDtypeStruct((FFT_N2, FFT_N1, D_MODEL), BF16),
                   jax.ShapeDtypeStruct((1, D_MODEL), F32)),
        grid=(FFT_N2 // FFT_NB,),
        in_specs=[
            pl.BlockSpec((HY_EMB_PAD, tr), lambda j: (0, j)),
            pl.BlockSpec((tr, 1), lambda j: (j, 0)),
            _const_spec((HY_WIDTH, HY_EMB_PAD)), col(),
            _const_spec((HY_WIDTH, HY_WIDTH)), col(),
            _const_spec((HY_WIDTH, HY_WIDTH)), col(),
            col(),
            _const_spec((HY_WIDTH, 2 * D_MODEL)),
            _const_spec((2, D_MODEL)),
        ],
        out_specs=(pl.BlockSpec((FFT_NB, FFT_N1, D_MODEL), lambda j: (j, 0, 0)),
                   pl.BlockSpec((1, D_MODEL), lambda j: (0, 0))),
        compiler_params=_cparams(("arbitrary",)), name="hyena_filter",
    )(feat_t, t_col, w1t, b1, w2t, b2, w3t, b3, freq, wout, decay)


def _select_column(tab_ref, idx):
    tab = tab_ref[...]
    lane = lax.broadcasted_iota(jnp.int32, tab.shape, 1)
    return jnp.sum(jnp.where(lane == idx, tab, 0.0), axis=-1, keepdims=True)


def _stack_bf16(re, im):
    return jnp.concatenate([re, im], axis=0).astype(BF16)


def _twiddled(cr, ci, c, s):
    return cr * c - ci * s, cr * s + ci * c


def _gather_rows(ref, lead, i, rows):
    slabs = ref.shape[len(lead)]
    parts = [ref[(*lead, c, pl.ds(i, rows, stride=FFT_NB), slice(None))] for c in range(slabs)]
    return jnp.concatenate(parts, axis=1)


def _scatter_rows(ref, i, val):
    rows = val.shape[0]
    for c in range(ref.shape[0]):
        ref[c, pl.ds(i, rows, stride=FFT_NB), :] = val[:, c * 128:(c + 1) * 128]


def _tile_rows(ref, q):
    slabs, groups = ref.shape[:2]
    parts = [ref[c, :, q * FFT_NB:(q + 1) * FFT_NB, :].reshape(groups * FFT_NB, 128) for c in range(slabs)]
    return jnp.concatenate(parts, axis=1)


def _fft1_kernel(s_ref, cr_ref, ci_ref, twc_ref, tws_ref, ar_ref, ai_ref):
    n2_0 = pl.program_id(1) * FFT_NB
    for i in range(FFT_NB):
        pr, pi = _twiddled(cr_ref[...], ci_ref[...],
                           _select_column(twc_ref, n2_0 + i), _select_column(tws_ref, n2_0 + i))
        m = jnp.concatenate([jnp.concatenate([pr, -pi], axis=1), jnp.concatenate([pi, pr], axis=1)], axis=0)
        rhs = _stack_bf16(_gather_rows(s_ref, (0,), i, FFT_H1), _gather_rows(s_ref, (1,), i, FFT_H1))
        a = jnp.dot(m.astype(BF16), rhs, preferred_element_type=F32)
        _scatter_rows(ar_ref, i, a[:FFT_N1])
        _scatter_rows(ai_ref, i, a[FFT_N1:])


def _fft1_filter_kernel(g_ref, cr_ref, ci_ref, twc_ref, tws_ref, ar_ref, ai_ref):
    n2_0 = pl.program_id(1) * FFT_NB
    for i in range(FFT_NB):
        pr, pi = _twiddled(cr_ref[...], ci_ref[...],
                           _select_column(twc_ref, n2_0 + i), _select_column(tws_ref, n2_0 + i))
        m = jnp.concatenate([pr, pi], axis=0).astype(BF16)
        a = jnp.dot(m, g_ref[i], preferred_element_type=F32)
        _scatter_rows(ar_ref, i, a[:FFT_N1])
        _scatter_rows(ai_ref, i, a[FFT_N1:])


def _fft_stage1(s5, g3, cr, ci, twc, tws):
    consts = [_const_spec(cr.shape), _const_spec(ci.shape),
              _const_spec((FFT_N1, FFT_N2)), _const_spec((FFT_N1, FFT_N2))]
    out = jax.ShapeDtypeStruct((N_SLABS, FFT_N2 // FFT_NB, FFT_N1 * FFT_NB, 128), F32)
    oblk = pl.BlockSpec((STEP_SLABS, None, FFT_N1 * FFT_NB, 128), lambda h, j: (h, j, 0, 0))
    common = dict(out_shape=(out, out), grid=(D_MODEL // FFT_LANES, FFT_N2 // FFT_NB), out_specs=(oblk, oblk),
                  compiler_params=_cparams(("parallel", "parallel")))
    if g3 is None:
        return pl.pallas_call(
            _fft1_kernel,
            in_specs=[pl.BlockSpec((2, STEP_SLABS, None, FFT_H1 * FFT_NB, 128),
                                   lambda h, j: (0, h, j, 0, 0))] + consts,
            name="fft_stage1", **common,
        )(s5, cr, ci, twc, tws)
    return pl.pallas_call(
        _fft1_filter_kernel,
        in_specs=[pl.BlockSpec((FFT_NB, FFT_N1, FFT_LANES), lambda h, j: (j, 0, h))] + consts,
        name="fft_stage1_filter", **common,
    )(g3, cr, ci, twc, tws)


def _fft2_kernel(ar_ref, ai_ref, gr_ref, gi_ref, m_ref, fr_ref, fi_ref, twc_ref, tws_ref, br_ref, bi_ref):
    k0 = pl.program_id(1) * FFT_NB
    lanes = ar_ref.shape[0] * 128
    rhs = jnp.concatenate(
        [_stack_bf16(_tile_rows(r, q), _tile_rows(i, q))
         for q in range(FFT_NB) for r, i in ((ar_ref, ai_ref), (gr_ref, gi_ref))], axis=1)
    zg = jnp.dot(m_ref[...], rhs, preferred_element_type=F32)
    for q in range(FFT_NB):
        z = zg[:, 2 * q * lanes:(2 * q + 1) * lanes]
        g = zg[:, (2 * q + 1) * lanes:(2 * q + 2) * lanes]
        zr, zi = z[:FFT_N2], z[FFT_N2:]
        gr, gi = g[:FFT_N2], g[FFT_N2:]
        y = _stack_bf16(zr * gr - zi * gi, zr * gi + zi * gr)
        pr, pi = _twiddled(fr_ref[...], -fi_ref[...],
                           _select_column(twc_ref, k0 + q), -_select_column(tws_ref, k0 + q))
        mc = jnp.concatenate([jnp.concatenate([pr, -pi], axis=1), jnp.concatenate([pi, pr], axis=1)], axis=0)
        bb = jnp.dot(mc.astype(BF16), y, preferred_element_type=F32)
        _scatter_rows(br_ref, q, bb[:FFT_N2])
        _scatter_rows(bi_ref, q, bb[FFT_N2:])


def _fft_stage2(a, ag, m2, fr, fi, twc_t, tws_t):
    blk = pl.BlockSpec((STEP_SLABS, FFT_N2 // FFT_NB, FFT_NB * FFT_NB, 128), lambda h, k: (h, 0, k, 0))
    out = jax.ShapeDtypeStruct((N_SLABS, FFT_N1 // FFT_NB, FFT_N2 * FFT_NB, 128), F32)
    oblk = pl.BlockSpec((STEP_SLABS, None, FFT_N2 * FFT_NB, 128), lambda h, k: (h, k, 0, 0))
    return pl.pallas_call(
        _fft2_kernel, out_shape=(out, out), grid=(D_MODEL // FFT_LANES, FFT_N1 // FFT_NB),
        in_specs=[blk, blk, blk, blk, _const_spec((2 * FFT_N2, 2 * FFT_N2)),
                  _const_spec((FFT_N2, FFT_N2)), _const_spec((FFT_N2, FFT_N2)),
                  _const_spec((FFT_N2, FFT_N1)), _const_spec((FFT_N2, FFT_N1))],
        out_specs=(oblk, oblk),
        compiler_params=_cparams(("parallel", "parallel")), name="fft_stage2",
    )(*a, *ag, m2, fr, fi, twc_t, tws_t)


def _fft3_kernel(br_ref, bi_ref, s_ref, skip_ref, sum_ref, m_ref, y_ref):
    skip = skip_ref[...]
    scale = 1.0 / (sum_ref[...] * FFT_N)
    for i in range(FFT_NB):
        rhs = _stack_bf16(_tile_rows(br_ref, i), _tile_rows(bi_ref, i))
        y = jnp.dot(m_ref[...], rhs, preferred_element_type=F32)
        for bb in range(BATCH):
            y_b = y[bb * FFT_H1:(bb + 1) * FFT_H1]
            _scatter_rows(y_ref.at[bb], i, y_b * scale + _gather_rows(s_ref, (bb,), i, FFT_H1) * skip)


def _fft_stage3(b, s5, skip, gsum, m3):
    act = pl.BlockSpec((BATCH, STEP_SLABS, None, FFT_H1 * FFT_NB, 128), lambda h, j: (0, h, j, 0, 0))
    bblk = pl.BlockSpec((STEP_SLABS, FFT_N1 // FFT_NB, FFT_NB * FFT_NB, 128), lambda h, j: (h, 0, j, 0))
    vec = pl.BlockSpec((1, FFT_LANES), lambda h, j: (0, h))
    return pl.pallas_call(
        _fft3_kernel, out_shape=jax.ShapeDtypeStruct(s5.shape, F32),
        grid=(D_MODEL // FFT_LANES, FFT_N2 // FFT_NB),
        in_specs=[bblk, bblk, act, vec, vec, _const_spec((FFT_N1, 2 * FFT_N1))],
        out_specs=act,
        compiler_params=_cparams(("parallel", "parallel")), name="fft_stage3",
    )(*b, s5, skip, gsum, m3)


def _swiglu_residual(x, g, wg_ref, wu_ref, wd_ref):
    h = _rms(x, g).astype(BF16)
    gate = jnp.dot(h, wg_ref[...], preferred_element_type=F32)
    up = jnp.dot(h, wu_ref[...], preferred_element_type=F32)
    a = (gate * jax.nn.sigmoid(gate) * up).astype(BF16)
    return x + jnp.dot(a, wd_ref[...], preferred_element_type=F32)


def _proj_ffn_kernel(x_ref, y_ref, x0_ref, wo_ref, bo_ref, g_ref, wg_ref, wu_ref, wd_ref, o_ref):
    y = (_load_grouped(y_ref) * x0_ref[...].astype(F32)).astype(BF16)
    x = x_ref[...] + jnp.dot(y, wo_ref[...], preferred_element_type=F32) + bo_ref[...]
    o_ref[...] = _swiglu_residual(x, g_ref[...], wg_ref, wu_ref, wd_ref)


def _proj_ffn_norm_kernel(x_ref, y_ref, wo_ref, bo_ref, g_ref, wg_ref, wu_ref, wd_ref, gf_ref, o_ref):
    x = x_ref[...] + jnp.dot(y_ref[...], wo_ref[...], preferred_element_type=F32) + bo_ref[...]
    o_ref[...] = _rms(_swiglu_residual(x, g_ref[...], wg_ref, wu_ref, wd_ref), gf_ref[...])


def _proj_ffn_weight_specs():
    return [_const_spec((D_MODEL, D_MODEL)), _const_spec((1, D_MODEL)), _const_spec((1, D_MODEL)),
            _const_spec((D_MODEL, FFN_HIDDEN)), _const_spec((D_MODEL, FFN_HIDDEN)),
            _const_spec((FFN_HIDDEN, D_MODEL))]


def _proj_ffn(x3, y5, x0, wo, bo, g, wg, wu, wd, tm=512):
    blk = pl.BlockSpec((None, tm, D_MODEL), lambda bb, i: (bb, i, 0))
    return pl.pallas_call(
        _proj_ffn_kernel, out_shape=jax.ShapeDtypeStruct(x3.shape, F32), grid=(BATCH, SEQ // tm),
        in_specs=[blk, _grouped_spec(tm), blk] + _proj_ffn_weight_specs(),
        out_specs=blk,
        compiler_params=_cparams(("parallel", "parallel")), name="proj_ffn",
    )(x3, y5, x0, wo, bo, g, wg, wu, wd)


def _proj_ffn_norm(x, y, wo, bo, g, wg, wu, wd, gf, tm=512):
    m = x.shape[0]
    blk = pl.BlockSpec((tm, D_MODEL), lambda i: (i, 0))
    return pl.pallas_call(
        _proj_ffn_norm_kernel, out_shape=jax.ShapeDtypeStruct((m, D_MODEL), F32), grid=(m // tm,),
        in_specs=[blk, blk] + _proj_ffn_weight_specs() + [_const_spec((1, D_MODEL))],
        out_specs=blk,
        compiler_params=_cparams(("parallel",)), name="proj_ffn_norm",
    )(x, y, wo, bo, g, wg, wu, wd, gf)


def _qkv_kernel(x_ref, g_ref, w_ref, b_ref, q_ref, k_ref, v_ref):
    h = _rms(x_ref[...], g_ref[...]).astype(BF16)
    z = jnp.dot(h, w_ref[...], preferred_element_type=F32) + b_ref[...]
    q_ref[...] = (z[:, :D_MODEL] * (NA_HEAD_DIM ** -0.5 * LOG2_E)).astype(BF16)
    k_ref[...] = z[:, D_MODEL:2 * D_MODEL].astype(BF16)
    v_ref[...] = z[:, 2 * D_MODEL:].astype(BF16)


def _qkv(x, g, w, b, tm=1024):
    m = x.shape[0]
    act = jax.ShapeDtypeStruct((m, D_MODEL), BF16)
    blk = pl.BlockSpec((tm, D_MODEL), lambda i: (i, 0))
    return pl.pallas_call(
        _qkv_kernel, out_shape=(act, act, act), grid=(m // tm,),
        in_specs=[blk, _const_spec((1, D_MODEL)), _const_spec((D_MODEL, 3 * D_MODEL)),
                  _const_spec((1, 3 * D_MODEL))],
        out_specs=(blk, blk, blk),
        compiler_params=_cparams(("parallel",)), name="na_qkv",
    )(x, g, w, b)


def _na_kernel(q_ref, k0_ref, k1_ref, k2_ref, v0_ref, v1_ref, v2_ref, bias_ref, o_ref):
    tok = NA_QR * GRID_W
    lane = lax.broadcasted_iota(jnp.int32, (1, 128), 1)
    first = lane < NA_HEAD_DIM
    ones = jnp.ones((NA_KR * GRID_W, 128), BF16)
    def logits(p):
        sl = slice(128 * p, 128 * (p + 1))
        qp = q_ref[:, sl]
        zero = jnp.zeros_like(qp)
        q2 = jnp.concatenate([jnp.where(first, qp, zero), jnp.where(first, zero, qp)], axis=0)
        kp = jnp.concatenate([k0_ref[:, sl], k1_ref[:, sl], k2_ref[:, sl]], axis=0)
        s = lax.dot_general(q2, kp, (((1,), (1,)), ((), ())), preferred_element_type=F32)
        return s + bias_ref[p]

    pending = [logits(p) for p in range(NA_LOGITS_AHEAD)]
    for p in range(NA_HEADS // 2):
        sl = slice(128 * p, 128 * (p + 1))
        s = pending.pop(0)
        if p + NA_LOGITS_AHEAD < NA_HEADS // 2:
            pending.append(logits(p + NA_LOGITS_AHEAD))
        vp = jnp.concatenate([v0_ref[:, sl], v1_ref[:, sl], v2_ref[:, sl]], axis=0)
        e = jnp.exp2(s - jnp.max(s, axis=-1, keepdims=True)).astype(BF16)
        o = jnp.dot(e, jnp.concatenate([vp, ones], axis=1), preferred_element_type=F32)
        o = o[:, :128] * (1.0 / o[:, 128:])
        o_ref[:, sl] = jnp.where(first, o[:tok], o[tok:]).astype(BF16)


def _na_attention(q, k, v, bias):
    tok = NA_QR * GRID_W
    bias = bias.reshape(3, NA_HEADS // 2, 2 * tok, NA_KR * GRID_W)

    def chunk(t):
        return pl.BlockSpec((None, None, tok, D_MODEL),
                            lambda bb, j: (bb, jnp.clip(j - 1, 0, NA_NBLK - 3) + t, 0, 0))

    def btype(bb, j):
        return (jnp.where(j == 0, 0, jnp.where(j == NA_NBLK - 1, 2, 1)), 0, 0, 0)

    qspec = pl.BlockSpec((None, None, tok, D_MODEL), lambda bb, j: (bb, j, 0, 0))
    return pl.pallas_call(
        _na_kernel, out_shape=jax.ShapeDtypeStruct(q.shape, BF16), grid=(BATCH, NA_NBLK),
        in_specs=[qspec, chunk(0), chunk(1), chunk(2), chunk(0), chunk(1), chunk(2),
                  pl.BlockSpec((None, NA_HEADS // 2, 2 * tok, NA_KR * GRID_W), btype,
                               pipeline_mode=pl.Buffered(1))],
        out_specs=qspec,
        compiler_params=_cparams(("parallel", "arbitrary")), name="na_attention",
    )(q, k, k, k, v, v, v, bias)


def kernel(x, norm_mix, norm_ffn, norm_final, hy_w_in, hy_b_in, hy_conv_w, hy_conv_b, hy_f_w1, hy_f_b1, hy_f_w2, hy_f_b2, hy_f_w3, hy_f_b3, hy_f_freq, hy_f_wout, hy_decay, hy_skip, hy_w_out, hy_b_out, na_w_qkv, na_b_qkv, na_rpb, na_w_o, na_b_o, ffn_w_gate, ffn_w_up, ffn_w_down):
    tabs = _fft_tables()
    row = lambda a: a.reshape(1, -1)
    m_tok = BATCH * SEQ

    s, x0 = _hyena_in(x, row(norm_mix[0]), hy_w_in[0].astype(BF16), row(hy_b_in[0]),
                      hy_conv_w[0], row(hy_conv_b[0]))
    col = lambda a: a.reshape(-1, 1)
    feat_t, t_col = _filter_features()
    w1t = jnp.zeros((HY_WIDTH, HY_EMB_PAD), F32).at[:, :HY_EMB_DIM].set(hy_f_w1[0].T)
    g_taps, g_sum = _hyena_filter(feat_t, t_col, w1t, col(hy_f_b1[0]), hy_f_w2[0].T, col(hy_f_b2[0]),
                                  hy_f_w3[0].T, col(hy_f_b3[0]), col(hy_f_freq[0]), hy_f_wout[0], hy_decay[0])
    bf = lambda name: jnp.asarray(tabs[name]).astype(BF16)
    ga = _fft_stage1(None, g_taps, tabs["cfr"], tabs["cfi"], tabs["twc"], tabs["tws"])
    sa = _fft_stage1(s, None, tabs["cr"], tabs["ci"], tabs["twc"], tabs["tws"])
    sb = _fft_stage2(sa, ga, bf("m2"), tabs["fr"], tabs["fi"], tabs["twc_t"], tabs["tws_t"])
    yv = _fft_stage3(sb, s, row(hy_skip[0]), g_sum, bf("m3"))
    xf = _proj_ffn(x, yv, x0, hy_w_out[0].astype(BF16), row(hy_b_out[0]),
                   row(norm_ffn[0]), ffn_w_gate[0].astype(BF16), ffn_w_up[0].astype(BF16),
                   ffn_w_down[0].astype(BF16)).reshape(m_tok, D_MODEL)

    q, k, v = _qkv(xf, row(norm_mix[1]), na_w_qkv[0].astype(BF16), row(na_b_qkv[0]))
    bias = _na_bias(na_rpb[0])
    blk = lambda a: a.reshape(BATCH, NA_NBLK, NA_QR * GRID_W, D_MODEL)
    att = _na_attention(blk(q), blk(k), blk(v), bias)
    xf = _proj_ffn_norm(xf, att.reshape(m_tok, D_MODEL), na_w_o[0].astype(BF16), row(na_b_o[0]),
                        row(norm_ffn[1]), ffn_w_gate[1].astype(BF16), ffn_w_up[1].astype(BF16),
                        ffn_w_down[1].astype(BF16), row(norm_final))
    return xf.reshape(BATCH, SEQ, D_MODEL)
```

```python
import functools
import math

import numpy as np
import jax
import jax.numpy as jnp
from jax import lax
from jax.experimental import pallas as pl
from jax.experimental.pallas import tpu as pltpu

F32 = jnp.float32
BF16 = jnp.bfloat16

D_MODEL = 1024
SEQ = 16384
BATCH = 2
FFN_HIDDEN = 2816
GRID_W = 64
RMS_EPS = 1e-6
HY_EMB_DIM = 33
HY_EMB_PAD = 40
HY_N_BANDS = 16
HY_WIDTH = 64
NA_HEADS = 16
NA_HEAD_DIM = 64
NA_KH = 8
NA_KW = 16

FFT_N = 2 * SEQ
FFT_N1 = 256
FFT_N2 = 128
FFT_H1 = FFT_N1 // 2
FFT_NB = 8
FFT_LANES = 512
N_SLABS = D_MODEL // 128
STEP_SLABS = FFT_LANES // 128

NA_ROWS = SEQ // GRID_W
NA_QR = 4
NA_KR = 12
NA_NBLK = NA_ROWS // NA_QR
NA_LOGITS_AHEAD = 2
NEG_BIG = -1e30
LOG2_E = math.log2(math.e)

VMEM_LIMIT = 56 * 1024 * 1024


def _cparams(sem):
    return pltpu.CompilerParams(dimension_semantics=sem, vmem_limit_bytes=VMEM_LIMIT)


def _const_spec(shape):
    nd = len(shape)
    return pl.BlockSpec(shape, lambda *_: (0,) * nd, pipeline_mode=pl.Buffered(1))


def _rms(x, g):
    ms = jnp.mean(x * x, axis=-1, keepdims=True)
    return x * lax.rsqrt(ms + RMS_EPS) * g


@functools.lru_cache(maxsize=None)
def _fft_tables():
    n1, n2, n = FFT_N1, FFT_N2, FFT_N
    k1 = np.arange(n1)[:, None]
    nh = np.arange(FFT_H1)[None, :]
    nf = np.arange(n1)[None, :]
    c = np.exp(-2j * np.pi * (nh * k1) / n1)
    cf = np.exp(-2j * np.pi * (nf * k1) / n1)
    e = np.exp(2j * np.pi * (nh * k1) / n1).T
    m3 = np.block([[e.real, -e.imag], [e.imag, e.real]])
    a = np.arange(n2)
    f = np.exp(-2j * np.pi * np.outer(a, a) / n2)
    m2 = np.block([[f.real, -f.imag], [f.imag, f.real]])
    tw = np.exp(-2j * np.pi * np.outer(np.arange(n1), np.arange(n2)) / n)
    as_f = lambda m: np.asarray(m, dtype=np.float32)
    return dict(cr=as_f(c.real), ci=as_f(c.imag), cfr=as_f(cf.real), cfi=as_f(cf.imag), m3=as_f(m3),
                m2=as_f(m2), fr=as_f(f.real), fi=as_f(f.imag),
                twc=as_f(tw.real), tws=as_f(tw.imag),
                twc_t=as_f(tw.real.T), tws_t=as_f(tw.imag.T))


@functools.lru_cache(maxsize=None)
def _filter_features():
    ell = SEQ
    t = np.linspace(0.0, 1.0, ell, dtype=np.float32).astype(np.float64)
    w = (2.0 * math.pi * np.arange(ell, dtype=np.float32) / ell).astype(np.float32).astype(np.float64)
    bands = np.linspace(1e-4, HY_N_BANDS - 1, HY_N_BANDS, dtype=np.float32).astype(np.float64)
    ang = (bands[None, :] * w[:, None]).astype(np.float32).astype(np.float64)
    feat = np.concatenate([t[:, None], np.cos(ang), -np.sin(ang)], axis=-1)
    tab = np.zeros((2 * ell, HY_EMB_PAD), np.float64)
    tab[:ell, :HY_EMB_DIM] = feat
    tab[:ell, HY_EMB_DIM] = 1.0
    tab[ell + 1:, :HY_EMB_DIM] = feat[:0:-1]
    tab[ell + 1:, HY_EMB_DIM] = 1.0
    j, d, i, h = np.meshgrid(np.arange(FFT_N2 // FFT_NB), np.arange(2), np.arange(FFT_NB), np.arange(FFT_H1),
                             indexing="ij")
    perm = (FFT_N2 * (d * FFT_H1 + h) + FFT_NB * j + i).reshape(-1)
    tab = tab[perm]
    return np.asarray(tab.T, dtype=np.float32), np.asarray(tab[:, 0:1], dtype=np.float32)


def _na_bias_plan():
    plan = []
    for j in (0, 1, NA_NBLK - 1):
        base = NA_QR * min(max(j - 1, 0), NA_NBLK - 3)
        per_type = []
        for ri in range(NA_QR):
            r = NA_QR * j + ri
            rs = min(max(r - NA_KH // 2, 0), NA_ROWS - NA_KH)
            row = []
            for m in range(NA_KR // 2):
                kr = base + 2 * m
                row.append((kr - r + (NA_KH - 1), 0 <= kr - rs < NA_KH, 0 <= kr + 1 - rs < NA_KH))
            per_type.append(row)
        plan.append(per_type)
    return plan


def _na_bias_kernel(v_ref, o_ref):
    n_e = 2 * NA_KH
    c = lax.broadcasted_iota(jnp.int32, (GRID_W, 128), 0)
    lane = lax.broadcasted_iota(jnp.int32, (GRID_W, 128), 1)
    kc = lane % GRID_W
    cs = jnp.clip(c - NA_KW // 2, 0, GRID_W - NA_KW)
    col_ok = (kc >= cs) & (kc < cs + NA_KW)
    left = lane < GRID_W
    tiles = []
    for e in range(n_e):
        x = jnp.broadcast_to(v_ref[e:e + 1, :] * LOG2_E, (GRID_W, 128))
        tiles.append(pltpu.roll(x, 128 - (NA_KW - 1), 1, stride=1, stride_axis=0))
    neg = jnp.full((GRID_W, 128), NEG_BIG, F32)
    for t, per_type in enumerate(_na_bias_plan()):
        for ri, row in enumerate(per_type):
            for m, (ro_left, ok_l, ok_r) in enumerate(row):
                if not (ok_l or ok_r):
                    tile = neg
                else:
                    mask = col_ok if (ok_l and ok_r) else (col_ok & left if ok_l else col_ok & ~left)
                    tile = jnp.where(mask, tiles[ro_left + 1], NEG_BIG)
                o_ref[t, ri * GRID_W:(ri + 1) * GRID_W, m * 128:(m + 1) * 128] = tile


def _na_bias(rpb):
    n_e = 2 * NA_KH
    rp = jnp.pad(rpb, ((0, 0), (1, 1), (0, GRID_W - (2 * NA_KW - 1))))
    v = jnp.concatenate([rp[:, :n_e], rp[:, 1:n_e + 1]], axis=-1)
    tok, keys = NA_QR * GRID_W, NA_KR * GRID_W
    return pl.pallas_call(
        _na_bias_kernel, out_shape=jax.ShapeDtypeStruct((3, NA_HEADS, tok, keys), F32), grid=(NA_HEADS,),
        in_specs=[pl.BlockSpec((None, n_e, 128), lambda h: (h, 0, 0))],
        out_specs=pl.BlockSpec((3, None, tok, keys), lambda h: (0, h, 0, 0)),
        compiler_params=_cparams(("parallel",)), name="na_bias",
    )(v)


def _hy_in_kernel(xp_ref, x_ref, xn_ref, g_ref, w_ref, b_ref, cw_ref, cb_ref,
                  s_ref, x0_ref, z_ref, *, tl, nt):
    i = pl.program_id(1)
    xh = jnp.concatenate([xp_ref[...], x_ref[...], xn_ref[...]], axis=0)
    h = _rms(xh, g_ref[...]).astype(BF16)
    b = b_ref[...]
    row = lax.broadcasted_iota(jnp.int32, (tl + 16, 1), 0)
    inside = ((row >= 8) | (i > 0)) & ((row < tl + 8) | (i < nt - 1))
    z = jnp.where(inside, jnp.dot(h, w_ref[...], preferred_element_type=F32), -b)
    z_ref[0] = pltpu.roll(z, 1, 0)
    z_ref[1] = z
    z_ref[2] = pltpu.roll(z, tl + 15, 0)
    bias = cb_ref[...] + b * (cw_ref[0:1, :] + cw_ref[1:2, :] + cw_ref[2:3, :])

    def conv(c0):
        sl = slice(c0, c0 + D_MODEL)
        rows = slice(8, 8 + tl)
        return (cw_ref[0:1, sl] * z_ref[0, rows, sl] + cw_ref[1:2, sl] * z_ref[1, rows, sl]
                + cw_ref[2:3, sl] * z_ref[2, rows, sl] + bias[:, sl])

    x0_ref[...] = conv(0).astype(BF16)
    _store_grouped(s_ref, conv(2 * D_MODEL) * conv(D_MODEL))


def _store_grouped(ref, val):
    n1s = val.shape[0] // FFT_N2
    for c in range(N_SLABS):
        for jg in range(FFT_N2 // FFT_NB):
            tiles = [val[a * FFT_N2 + jg * FFT_NB:a * FFT_N2 + (jg + 1) * FFT_NB, c * 128:(c + 1) * 128]
                     for a in range(n1s)]
            ref[c, jg] = jnp.concatenate(tiles, axis=0)


def _load_grouped(ref):
    n1s = ref.shape[2] // FFT_NB
    slabs = []
    for c in range(N_SLABS):
        tiles = [ref[c, jg, a * FFT_NB:(a + 1) * FFT_NB, :] for a in range(n1s)
                 for jg in range(FFT_N2 // FFT_NB)]
        slabs.append(jnp.concatenate(tiles, axis=0))
    return jnp.concatenate(slabs, axis=1)


def _grouped_spec(tl):
    rows = tl // FFT_N2 * FFT_NB
    return pl.BlockSpec((None, N_SLABS, FFT_N2 // FFT_NB, rows, 128), lambda bb, i: (bb, 0, 0, i, 0))


def _hyena_in(x, g, w, b, cw, cb, tl=512):
    nt = SEQ // tl
    t8 = tl // 8
    kern = functools.partial(_hy_in_kernel, tl=tl, nt=nt)
    s_out = jax.ShapeDtypeStruct((BATCH, N_SLABS, FFT_N2 // FFT_NB, FFT_H1 * FFT_NB, 128), F32)
    x0_out = jax.ShapeDtypeStruct((BATCH, SEQ, D_MODEL), BF16)
    return pl.pallas_call(
        kern, out_shape=(s_out, x0_out), grid=(BATCH, nt),
        in_specs=[
            pl.BlockSpec((None, 8, D_MODEL), lambda bb, i: (bb, jnp.maximum(i * t8 - 1, 0), 0)),
            pl.BlockSpec((None, tl, D_MODEL), lambda bb, i: (bb, i, 0)),
            pl.BlockSpec((None, 8, D_MODEL), lambda bb, i: (bb, jnp.minimum((i + 1) * t8, SEQ // 8 - 1), 0)),
            _const_spec((1, D_MODEL)), _const_spec((D_MODEL, 3 * D_MODEL)), _const_spec((1, 3 * D_MODEL)),
            _const_spec((3, 3 * D_MODEL)), _const_spec((1, 3 * D_MODEL)),
        ],
        out_specs=(_grouped_spec(tl), pl.BlockSpec((None, tl, D_MODEL), lambda bb, i: (bb, i, 0))),
        scratch_shapes=[pltpu.VMEM((3, tl + 16, 3 * D_MODEL), F32)],
        compiler_params=_cparams(("parallel", "parallel")), name="hyena_in",
    )(x, x, x, g, w, b, cw, cb)


def _dot3(a, b, dims):
    a_hi = a.astype(BF16)
    a_lo = (a - a_hi.astype(F32)).astype(BF16)
    b_hi = b.astype(BF16)
    b_lo = (b - b_hi.astype(F32)).astype(BF16)
    dot = lambda u, v: lax.dot_general(u, v, dims, preferred_element_type=F32)
    return dot(a_hi, b_hi) + dot(a_hi, b_lo) + dot(a_lo, b_hi)


def _filter_kernel(ft_ref, tc_ref, w1_ref, b1_ref, w2_ref, b2_ref, w3_ref, b3_ref, fr_ref, wo_ref, dec_ref,
                   g_ref, sum_ref):
    hp = lax.Precision.HIGHEST
    ft = ft_ref[...]
    fr = fr_ref[...]
    h = jnp.sin(fr * (jnp.dot(w1_ref[...], ft, precision=hp, preferred_element_type=F32) + b1_ref[...]))
    h = jnp.sin(fr * (jnp.dot(w2_ref[...], h, precision=hp, preferred_element_type=F32) + b2_ref[...]))
    h = jnp.sin(fr * (jnp.dot(w3_ref[...], h, precision=hp, preferred_element_type=F32) + b3_ref[...]))
    h = h * ft[HY_EMB_DIM:HY_EMB_DIM + 1, :]
    half = FFT_NB * FFT_H1
    total = jnp.zeros((1, D_MODEL), F32)
    for d in range(2):
        hd = h[:, d * half:(d + 1) * half]
        o = _dot3(hd, wo_ref[:, d * D_MODEL:(d + 1) * D_MODEL], (((0,), (0,)), ((), ())))
        t = tc_ref[d * half:(d + 1) * half, :]
        gq = o * jnp.exp(-t * jnp.abs(dec_ref[d:d + 1, :]))
        total = total + jnp.sum(jnp.abs(gq), axis=0, keepdims=True)
        for i in range(FFT_NB):
            g_ref[i, d * FFT_H1:(d + 1) * FFT_H1, :] = gq[i * FFT_H1:(i + 1) * FFT_H1].astype(BF16)

    @pl.when(pl.program_id(0) == 0)
    def _():
        sum_ref[...] = jnp.zeros_like(sum_ref)

    sum_ref[...] += total


def _hyena_filter(feat_t, t_col, w1t, b1, w2t, b2, w3t, b3, freq, wout, decay):
    tr = 2 * FFT_NB * FFT_H1
    col = lambda: _const_spec((HY_WIDTH, 1))
    return pl.pallas_call(
        _filter_kernel,
        out_shape=(jax.ShapeDtypeStruct((FFT_N2, FFT_N1, D_MODEL), BF16),
                   jax.ShapeDtypeStruct((1, D_MODEL), F32)),
        grid=(FFT_N2 // FFT_NB,),
        in_specs=[
            pl.BlockSpec((HY_EMB_PAD, tr), lambda j: (0, j)),
            pl.BlockSpec((tr, 1), lambda j: (j, 0)),
            _const_spec((HY_WIDTH, HY_EMB_PAD)), col(),
            _const_spec((HY_WIDTH, HY_WIDTH)), col(),
            _const_spec((HY_WIDTH, HY_WIDTH)), col(),
            col(),
            _const_spec((HY_WIDTH, 2 * D_MODEL)),
            _const_spec((2, D_MODEL)),
        ],
        out_specs=(pl.BlockSpec((FFT_NB, FFT_N1, D_MODEL), lambda j: (j, 0, 0)),
                   pl.BlockSpec((1, D_MODEL), lambda j: (0, 0))),
        compiler_params=_cparams(("arbitrary",)), name="hyena_filter",
    )(feat_t, t_col, w1t, b1, w2t, b2, w3t, b3, freq, wout, decay)


def _select_column(tab_ref, idx):
    tab = tab_ref[...]
    lane = lax.broadcasted_iota(jnp.int32, tab.shape, 1)
    return jnp.sum(jnp.where(lane == idx, tab, 0.0), axis=-1, keepdims=True)


def _stack_bf16(re, im):
    return jnp.concatenate([re, im], axis=0).astype(BF16)


def _twiddled(cr, ci, c, s):
    return cr * c - ci * s, cr * s + ci * c


def _gather_rows(ref, lead, i, rows):
    slabs = ref.shape[len(lead)]
    parts = [ref[(*lead, c, pl.ds(i, rows, stride=FFT_NB), slice(None))] for c in range(slabs)]
    return jnp.concatenate(parts, axis=1)


def _scatter_rows(ref, i, val):
    rows = val.shape[0]
    for c in range(ref.shape[0]):
        ref[c, pl.ds(i, rows, stride=FFT_NB), :] = val[:, c * 128:(c + 1) * 128]


def _tile_rows(ref, q):
    slabs, groups = ref.shape[:2]
    parts = [ref[c, :, q * FFT_NB:(q + 1) * FFT_NB, :].reshape(groups * FFT_NB, 128) for c in range(slabs)]
    return jnp.concatenate(parts, axis=1)


def _fft1_kernel(s_ref, cr_ref, ci_ref, twc_ref, tws_ref, ar_ref, ai_ref):
    n2_0 = pl.program_id(1) * FFT_NB
    for i in range(FFT_NB):
        pr, pi = _twiddled(cr_ref[...], ci_ref[...],
                           _select_column(twc_ref, n2_0 + i), _select_column(tws_ref, n2_0 + i))
        m = jnp.concatenate([jnp.concatenate([pr, -pi], axis=1), jnp.concatenate([pi, pr], axis=1)], axis=0)
        rhs = _stack_bf16(_gather_rows(s_ref, (0,), i, FFT_H1), _gather_rows(s_ref, (1,), i, FFT_H1))
        a = jnp.dot(m.astype(BF16), rhs, preferred_element_type=F32)
        _scatter_rows(ar_ref, i, a[:FFT_N1])
        _scatter_rows(ai_ref, i, a[FFT_N1:])


def _fft1_filter_kernel(g_ref, cr_ref, ci_ref, twc_ref, tws_ref, ar_ref, ai_ref):
    n2_0 = pl.program_id(1) * FFT_NB
    for i in range(FFT_NB):
        pr, pi = _twiddled(cr_ref[...], ci_ref[...],
                           _select_column(twc_ref, n2_0 + i), _select_column(tws_ref, n2_0 + i))
        m = jnp.concatenate([pr, pi], axis=0).astype(BF16)
        a = jnp.dot(m, g_ref[i], preferred_element_type=F32)
        _scatter_rows(ar_ref, i, a[:FFT_N1])
        _scatter_rows(ai_ref, i, a[FFT_N1:])


def _fft_stage1(s5, g3, cr, ci, twc, tws):
    consts = [_const_spec(cr.shape), _const_spec(ci.shape),
              _const_spec((FFT_N1, FFT_N2)), _const_spec((FFT_N1, FFT_N2))]
    out = jax.ShapeDtypeStruct((N_SLABS, FFT_N2 // FFT_NB, FFT_N1 * FFT_NB, 128), F32)
    oblk = pl.BlockSpec((STEP_SLABS, None, FFT_N1 * FFT_NB, 128), lambda h, j: (h, j, 0, 0))
    common = dict(out_shape=(out, out), grid=(D_MODEL // FFT_LANES, FFT_N2 // FFT_NB), out_specs=(oblk, oblk),
                  compiler_params=_cparams(("parallel", "parallel")))
    if g3 is None:
        return pl.pallas_call(
            _fft1_kernel,
            in_specs=[pl.BlockSpec((2, STEP_SLABS, None, FFT_H1 * FFT_NB, 128),
                                   lambda h, j: (0, h, j, 0, 0))] + consts,
            name="fft_stage1", **common,
        )(s5, cr, ci, twc, tws)
    return pl.pallas_call(
        _fft1_filter_kernel,
        in_specs=[pl.BlockSpec((FFT_NB, FFT_N1, FFT_LANES), lambda h, j: (j, 0, h))] + consts,
        name="fft_stage1_filter", **common,
    )(g3, cr, ci, twc, tws)


def _fft2_kernel(ar_ref, ai_ref, gr_ref, gi_ref, m_ref, fr_ref, fi_ref, twc_ref, tws_ref, br_ref, bi_ref):
    k0 = pl.program_id(1) * FFT_NB
    lanes = ar_ref.shape[0] * 128
    rhs = jnp.concatenate(
        [_stack_bf16(_tile_rows(r, q), _tile_rows(i, q))
         for q in range(FFT_NB) for r, i in ((ar_ref, ai_ref), (gr_ref, gi_ref))], axis=1)
    zg = jnp.dot(m_ref[...], rhs, preferred_element_type=F32)
    for q in range(FFT_NB):
        z = zg[:, 2 * q * lanes:(2 * q + 1) * lanes]
        g = zg[:, (2 * q + 1) * lanes:(2 * q + 2) * lanes]
        zr, zi = z[:FFT_N2], z[FFT_N2:]
        gr, gi = g[:FFT_N2], g[FFT_N2:]
        y = _stack_bf16(zr * gr - zi * gi, zr * gi + zi * gr)
        pr, pi = _twiddled(fr_ref[...], -fi_ref[...],
                           _select_column(twc_ref, k0 + q), -_select_column(tws_ref, k0 + q))
        mc = jnp.concatenate([jnp.concatenate([pr, -pi], axis=1), jnp.concatenate([pi, pr], axis=1)], axis=0)
        bb = jnp.dot(mc.astype(BF16), y, preferred_element_type=F32)
        _scatter_rows(br_ref, q, bb[:FFT_N2])
        _scatter_rows(bi_ref, q, bb[FFT_N2:])


def _fft_stage2(a, ag, m2, fr, fi, twc_t, tws_t):
    blk = pl.BlockSpec((STEP_SLABS, FFT_N2 // FFT_NB, FFT_NB * FFT_NB, 128), lambda h, k: (h, 0, k, 0))
    out = jax.ShapeDtypeStruct((N_SLABS, FFT_N1 // FFT_NB, FFT_N2 * FFT_NB, 128), F32)
    oblk = pl.BlockSpec((STEP_SLABS, None, FFT_N2 * FFT_NB, 128), lambda h, k: (h, k, 0, 0))
    return pl.pallas_call(
        _fft2_kernel, out_shape=(out, out), grid=(D_MODEL // FFT_LANES, FFT_N1 // FFT_NB),
        in_specs=[blk, blk, blk, blk, _const_spec((2 * FFT_N2, 2 * FFT_N2)),
                  _const_spec((FFT_N2, FFT_N2)), _const_spec((FFT_N2, FFT_N2)),
                  _const_spec((FFT_N2, FFT_N1)), _const_spec((FFT_N2, FFT_N1))],
        out_specs=(oblk, oblk),
        compiler_params=_cparams(("parallel", "parallel")), name="fft_stage2",
    )(*a, *ag, m2, fr, fi, twc_t, tws_t)


def _fft3_kernel(br_ref, bi_ref, s_ref, skip_ref, sum_ref, m_ref, y_ref):
    skip = skip_ref[...]
    scale = 1.0 / (sum_ref[...] * FFT_N)
    for i in range(FFT_NB):
        rhs = _stack_bf16(_tile_rows(br_ref, i), _tile_rows(bi_ref, i))
        y = jnp.dot(m_ref[...], rhs, preferred_element_type=F32)
        for bb in range(BATCH):
            y_b = y[bb * FFT_H1:(bb + 1) * FFT_H1]
            _scatter_rows(y_ref.at[bb], i, y_b * scale + _gather_rows(s_ref, (bb,), i, FFT_H1) * skip)


def _fft_stage3(b, s5, skip, gsum, m3):
    act = pl.BlockSpec((BATCH, STEP_SLABS, None, FFT_H1 * FFT_NB, 128), lambda h, j: (0, h, j, 0, 0))
    bblk = pl.BlockSpec((STEP_SLABS, FFT_N1 // FFT_NB, FFT_NB * FFT_NB, 128), lambda h, j: (h, 0, j, 0))
    vec = pl.BlockSpec((1, FFT_LANES), lambda h, j: (0, h))
    return pl.pallas_call(
        _fft3_kernel, out_shape=jax.ShapeDtypeStruct(s5.shape, F32),
        grid=(D_MODEL // FFT_LANES, FFT_N2 // FFT_NB),
        in_specs=[bblk, bblk, act, vec, vec, _const_spec((FFT_N1, 2 * FFT_N1))],
        out_specs=act,
        compiler_params=_cparams(("parallel", "parallel")), name="fft_stage3",
    )(*b, s5, skip, gsum, m3)


def _swiglu_residual(x, g, wg_ref, wu_ref, wd_ref):
    h = _rms(x, g).astype(BF16)
    gate = jnp.dot(h, wg_ref[...], preferred_element_type=F32)
    up = jnp.dot(h, wu_ref[...], preferred_element_type=F32)
    a = (gate * jax.nn.sigmoid(gate) * up).astype(BF16)
    return x + jnp.dot(a, wd_ref[...], preferred_element_type=F32)


def _proj_ffn_kernel(x_ref, y_ref, x0_ref, wo_ref, bo_ref, g_ref, wg_ref, wu_ref, wd_ref, o_ref):
    y = (_load_grouped(y_ref) * x0_ref[...].astype(F32)).astype(BF16)
    x = x_ref[...] + jnp.dot(y, wo_ref[...], preferred_element_type=F32) + bo_ref[...]
    o_ref[...] = _swiglu_residual(x, g_ref[...], wg_ref, wu_ref, wd_ref)


def _proj_ffn_norm_kernel(x_ref, y_ref, wo_ref, bo_ref, g_ref, wg_ref, wu_ref, wd_ref, gf_ref, o_ref):
    x = x_ref[...] + jnp.dot(y_ref[...], wo_ref[...], preferred_element_type=F32) + bo_ref[...]
    o_ref[...] = _rms(_swiglu_residual(x, g_ref[...], wg_ref, wu_ref, wd_ref), gf_ref[...])


def _proj_ffn_weight_specs():
    return [_const_spec((D_MODEL, D_MODEL)), _const_spec((1, D_MODEL)), _const_spec((1, D_MODEL)),
            _const_spec((D_MODEL, FFN_HIDDEN)), _const_spec((D_MODEL, FFN_HIDDEN)),
            _const_spec((FFN_HIDDEN, D_MODEL))]


def _proj_ffn(x3, y5, x0, wo, bo, g, wg, wu, wd, tm=512):
    blk = pl.BlockSpec((None, tm, D_MODEL), lambda bb, i: (bb, i, 0))
    return pl.pallas_call(
        _proj_ffn_kernel, out_shape=jax.ShapeDtypeStruct(x3.shape, F32), grid=(BATCH, SEQ // tm),
        in_specs=[blk, _grouped_spec(tm), blk] + _proj_ffn_weight_specs(),
        out_specs=blk,
        compiler_params=_cparams(("parallel", "parallel")), name="proj_ffn",
    )(x3, y5, x0, wo, bo, g, wg, wu, wd)


def _proj_ffn_norm(x, y, wo, bo, g, wg, wu, wd, gf, tm=512):
    m = x.shape[0]
    blk = pl.BlockSpec((tm, D_MODEL), lambda i: (i, 0))
    return pl.pallas_call(
        _proj_ffn_norm_kernel, out_shape=jax.ShapeDtypeStruct((m, D_MODEL), F32), grid=(m // tm,),
        in_specs=[blk, blk] + _proj_ffn_weight_specs() + [_const_spec((1, D_MODEL))],
        out_specs=blk,
        compiler_params=_cparams(("parallel",)), name="proj_ffn_norm",
    )(x, y, wo, bo, g, wg, wu, wd, gf)


def _qkv_kernel(x_ref, g_ref, w_ref, b_ref, q_ref, k_ref, v_ref):
    h = _rms(x_ref[...], g_ref[...]).astype(BF16)
    z = jnp.dot(h, w_ref[...], preferred_element_type=F32) + b_ref[...]
    q_ref[...] = (z[:, :D_MODEL] * (NA_HEAD_DIM ** -0.5 * LOG2_E)).astype(BF16)
    k_ref[...] = z[:, D_MODEL:2 * D_MODEL].astype(BF16)
    v_ref[...] = z[:, 2 * D_MODEL:].astype(BF16)


def _qkv(x, g, w, b, tm=1024):
    m = x.shape[0]
    act = jax.ShapeDtypeStruct((m, D_MODEL), BF16)
    blk = pl.BlockSpec((tm, D_MODEL), lambda i: (i, 0))
    return pl.pallas_call(
        _qkv_kernel, out_shape=(act, act, act), grid=(m // tm,),
        in_specs=[blk, _const_spec((1, D_MODEL)), _const_spec((D_MODEL, 3 * D_MODEL)),
                  _const_spec((1, 3 * D_MODEL))],
        out_specs=(blk, blk, blk),
        compiler_params=_cparams(("parallel",)), name="na_qkv",
    )(x, g, w, b)


def _na_kernel(q_ref, k0_ref, k1_ref, k2_ref, v0_ref, v1_ref, v2_ref, bias_ref, o_ref):
    tok = NA_QR * GRID_W
    lane = lax.broadcasted_iota(jnp.int32, (1, 128), 1)
    first = lane < NA_HEAD_DIM
    ones = jnp.ones((NA_KR * GRID_W, 128), BF16)
    def logits(p):
        sl = slice(128 * p, 128 * (p + 1))
        qp = q_ref[:, sl]
        zero = jnp.zeros_like(qp)
        q2 = jnp.concatenate([jnp.where(first, qp, zero), jnp.where(first, zero, qp)], axis=0)
        kp = jnp.concatenate([k0_ref[:, sl], k1_ref[:, sl], k2_ref[:, sl]], axis=0)
        s = lax.dot_general(q2, kp, (((1,), (1,)), ((), ())), preferred_element_type=F32)
        return s + bias_ref[p]

    pending = [logits(p) for p in range(NA_LOGITS_AHEAD)]
    for p in range(NA_HEADS // 2):
        sl = slice(128 * p, 128 * (p + 1))
        s = pending.pop(0)
        if p + NA_LOGITS_AHEAD < NA_HEADS // 2:
            pending.append(logits(p + NA_LOGITS_AHEAD))
        vp = jnp.concatenate([v0_ref[:, sl], v1_ref[:, sl], v2_ref[:, sl]], axis=0)
        e = jnp.exp2((s - jnp.max(s, axis=-1, keepdims=True)).astype(BF16))
        o = jnp.dot(e, jnp.concatenate([vp, ones], axis=1), preferred_element_type=F32)
        o = o[:, :128] * (1.0 / o[:, 128:])
        o_ref[:, sl] = jnp.where(first, o[:tok], o[tok:]).astype(BF16)


def _na_attention(q, k, v, bias):
    tok = NA_QR * GRID_W
    bias = bias.reshape(3, NA_HEADS // 2, 2 * tok, NA_KR * GRID_W)

    def chunk(t):
        return pl.BlockSpec((None, None, tok, D_MODEL),
                            lambda bb, j: (bb, jnp.clip(j - 1, 0, NA_NBLK - 3) + t, 0, 0))

    def btype(bb, j):
        return (jnp.where(j == 0, 0, jnp.where(j == NA_NBLK - 1, 2, 1)), 0, 0, 0)

    qspec = pl.BlockSpec((None, None, tok, D_MODEL), lambda bb, j: (bb, j, 0, 0))
    return pl.pallas_call(
        _na_kernel, out_shape=jax.ShapeDtypeStruct(q.shape, BF16), grid=(BATCH, NA_NBLK),
        in_specs=[qspec, chunk(0), chunk(1), chunk(2), chunk(0), chunk(1), chunk(2),
                  pl.BlockSpec((None, NA_HEADS // 2, 2 * tok, NA_KR * GRID_W), btype,
                               pipeline_mode=pl.Buffered(1))],
        out_specs=qspec,
        compiler_params=_cparams(("parallel", "arbitrary")), name="na_attention",
    )(q, k, k, k, v, v, v, bias)


def kernel(x, norm_mix, norm_ffn, norm_final, hy_w_in, hy_b_in, hy_conv_w, hy_conv_b, hy_f_w1, hy_f_b1, hy_f_w2, hy_f_b2, hy_f_w3, hy_f_b3, hy_f_freq, hy_f_wout, hy_decay, hy_skip, hy_w_out, hy_b_out, na_w_qkv, na_b_qkv, na_rpb, na_w_o, na_b_o, ffn_w_gate, ffn_w_up, ffn_w_down):
    tabs = _fft_tables()
    row = lambda a: a.reshape(1, -1)
    m_tok = BATCH * SEQ

    s, x0 = _hyena_in(x, row(norm_mix[0]), hy_w_in[0].astype(BF16), row(hy_b_in[0]),
                      hy_conv_w[0], row(hy_conv_b[0]))
    col = lambda a: a.reshape(-1, 1)
    feat_t, t_col = _filter_features()
    w1t = jnp.zeros((HY_WIDTH, HY_EMB_PAD), F32).at[:, :HY_EMB_DIM].set(hy_f_w1[0].T)
    g_taps, g_sum = _hyena_filter(feat_t, t_col, w1t, col(hy_f_b1[0]), hy_f_w2[0].T, col(hy_f_b2[0]),
                                  hy_f_w3[0].T, col(hy_f_b3[0]), col(hy_f_freq[0]), hy_f_wout[0], hy_decay[0])
    bf = lambda name: jnp.asarray(tabs[name]).astype(BF16)
    ga = _fft_stage1(None, g_taps, tabs["cfr"], tabs["cfi"], tabs["twc"], tabs["tws"])
    sa = _fft_stage1(s, None, tabs["cr"], tabs["ci"], tabs["twc"], tabs["tws"])
    sb = _fft_stage2(sa, ga, bf("m2"), tabs["fr"], tabs["fi"], tabs["twc_t"], tabs["tws_t"])
    yv = _fft_stage3(sb, s, row(hy_skip[0]), g_sum, bf("m3"))
    xf = _proj_ffn(x, yv, x0, hy_w_out[0].astype(BF16), row(hy_b_out[0]),
                   row(norm_ffn[0]), ffn_w_gate[0].astype(BF16), ffn_w_up[0].astype(BF16),
                   ffn_w_down[0].astype(BF16)).reshape(m_tok, D_MODEL)

    q, k, v = _qkv(xf, row(norm_mix[1]), na_w_qkv[0].astype(BF16), row(na_b_qkv[0]))
    bias = _na_bias(na_rpb[0])
    blk = lambda a: a.reshape(BATCH, NA_NBLK, NA_QR * GRID_W, D_MODEL)
    att = _na_attention(blk(q), blk(k), blk(v), bias)
    xf = _proj_ffn_norm(xf, att.reshape(m_tok, D_MODEL), na_w_o[0].astype(BF16), row(na_b_o[0]),
                        row(norm_ffn[1]), ffn_w_gate[1].astype(BF16), ffn_w_up[1].astype(BF16),
                        ffn_w_down[1].astype(BF16), row(norm_final))
    return xf.reshape(BATCH, SEQ, D_MODEL)
```
